```python
import math
import jax
import jax.numpy as jnp
from jax import lax
import numpy as np

D_MODEL = 2048
BATCH = 4
SEQ = 2048
DEPTH = 2
DEC_BATCH = 32
DEC_SEQ = 4
PAST_LEN = 8192
PAGE_SIZE = 128

HEAD_DIM = 128
CONV_CH = D_MODEL // 4
CONV_WIDTH = 3
MOBA_HEADS = (3 * D_MODEL // 8) // HEAD_DIM
MOBA_BLOCK = 256
MOBA_TOPK = 3
MOBA_Q_BLOCK = 32
DIFF_HEADS = (3 * D_MODEL // 8) // HEAD_DIM
DIFF_QK_DIM = HEAD_DIM // 2
DIFF_V_DIM = HEAD_DIM
DIFF_Q_BLOCK = 128
MIX_WIDTH = CONV_CH + MOBA_HEADS * HEAD_DIM + DIFF_HEADS * DIFF_V_DIM
IN_WIDTH = 3 * CONV_CH + 3 * MOBA_HEADS * HEAD_DIM + 2 * DIFF_HEADS * 2 * DIFF_QK_DIM + DIFF_HEADS * DIFF_V_DIM
PEER_HEADS = 8
PEER_N_KEYS = 128
PEER_EXPERTS = PEER_N_KEYS * PEER_N_KEYS
PEER_QDIM = 256
PEER_HALF = PEER_QDIM // 2
PEER_TOPK = 16
PEER_TOKEN_BLOCK = 128
EPS = 1e-6
NEG = -1e30

kernel_name = 'moba_diff_shortconv_peer_hybrid'


def rmsnorm(x, g):
    xf = x.astype(jnp.float32)
    y = xf * lax.rsqrt(jnp.mean(xf * xf, axis=-1, keepdims=True) + EPS) * g.astype(jnp.float32)
    return y.astype(x.dtype)


def map_query_blocks(fn, q, q_pos, block):
    B, T = q.shape[0], q.shape[1]
    c = min(block, T)
    n = -(-T // c)
    pad = n * c - T
    q = jnp.pad(q, [(0, 0), (0, pad)] + [(0, 0)] * (q.ndim - 2))
    pos = jnp.pad(q_pos, (0, pad), mode='edge')
    qb = jnp.moveaxis(q.reshape((B, n, c) + q.shape[2:]), 1, 0)
    pb = pos.reshape(n, c)
    out = lax.map(lambda a: fn(a[0], a[1]), (qb, pb))
    out = jnp.moveaxis(out, 0, 1)
    return out.reshape((B, n * c) + out.shape[3:])[:, :T]


def short_conv(b_gate, c_gate, x_in, conv_w, conv_state):
    u = c_gate * x_in
    u_ext = jnp.concatenate([conv_state.astype(u.dtype), u], axis=1)
    T = u.shape[1]
    y = sum(conv_w[j] * u_ext[:, j:j + T] for j in range(CONV_WIDTH))
    return b_gate * y, u_ext[:, T:]


def moba_attend(q, q_pos, k, v):
    B, L, H, D = k.shape
    nblk = -(-L // MOBA_BLOCK)
    pad = nblk * MOBA_BLOCK - L
    kb = jnp.pad(k, ((0, 0), (0, pad), (0, 0), (0, 0))).reshape(B, nblk, MOBA_BLOCK, H, D).transpose(0, 3, 1, 2, 4)
    vb = jnp.pad(v, ((0, 0), (0, pad), (0, 0), (0, 0))).reshape(B, nblk, MOBA_BLOCK, H, D).transpose(0, 3, 1, 2, 4)
    k_mean = jnp.mean(kb.astype(jnp.float32), axis=3)
    top = min(MOBA_TOPK, nblk)
    b_idx = jnp.arange(B)[:, None, None, None]
    h_idx = jnp.arange(H)[None, :, None, None]
    scale = D ** -0.5

    def block_fn(qc, pc):
        c = qc.shape[1]
        qf = qc.astype(jnp.float32)
        own = pc // MOBA_BLOCK
        gate = jnp.einsum('bqhd,bhnd->bhqn', qf, k_mean)
        fully_past = jnp.arange(nblk)[None, :] < own[:, None]
        gate = jnp.where(fully_past, gate, NEG)
        _, sel = lax.top_k(gate, top)
        own_b = jnp.broadcast_to(own[None, None, :, None], (B, H, c, 1))
        blocks = jnp.concatenate([sel, own_b], axis=-1)
        ok = jnp.concatenate([sel < own_b, jnp.ones((B, H, c, 1), bool)], axis=-1)
        kg = kb[b_idx, h_idx, blocks]
        vg = vb[b_idx, h_idx, blocks]
        key_pos = blocks[..., None] * MOBA_BLOCK + jnp.arange(MOBA_BLOCK)
        mask = ok[..., None] & (key_pos <= pc[None, None, :, None, None])
        s = jnp.einsum('bqhd,bhqjkd->bhqjk', qf, kg.astype(jnp.float32)) * scale
        s = jnp.where(mask, s, NEG).reshape(B, H, c, -1)
        p = jax.nn.softmax(s, axis=-1).reshape(B, H, c, top + 1, MOBA_BLOCK)
        out = jnp.einsum('bhqjk,bhqjkd->bqhd', p, vg.astype(jnp.float32))
        return out.astype(qc.dtype)

    return map_query_blocks(block_fn, q, q_pos, MOBA_Q_BLOCK)


def diff_attend(q, q_pos, k, v, lam):
    L = k.shape[1]
    key_pos = jnp.arange(L)
    kf = k.astype(jnp.float32)
    vf = v.astype(jnp.float32)
    scale = DIFF_QK_DIM ** -0.5

    def block_fn(qc, pc):
        s = jnp.einsum('bqhcd,bkhcd->bhcqk', qc.astype(jnp.float32), kf) * scale
        s = jnp.where(key_pos[None, :] <= pc[:, None], s, NEG)
        p = jax.nn.softmax(s, axis=-1)
        a = p[:, :, 0] - lam * p[:, :, 1]
        out = jnp.einsum('bhqk,bkhd->bqhd', a, vf)
        return out.astype(qc.dtype)

    return map_query_blocks(block_fn, q, q_pos, DIFF_Q_BLOCK)


def token_mix(h, mk_past, mv_past, dk_past, dv_past, conv_state, w_in, conv_w, lq1, lk1, lq2, lk2, lam_init, subln_g, w_out):
    B, T, _ = h.shape
    past_len = mk_past.shape[1]
    q_pos = past_len + jnp.arange(T, dtype=jnp.int32)
    proj = h @ w_in
    widths = (CONV_CH,) * 3 + (MOBA_HEADS * HEAD_DIM,) * 3 + (DIFF_HEADS * 2 * DIFF_QK_DIM,) * 2 + (DIFF_HEADS * DIFF_V_DIM,)
    points = [sum(widths[:i + 1]) for i in range(len(widths) - 1)]
    cb, cc, cx, mq, mk, mv, dq, dk, dv = jnp.split(proj, points, axis=-1)

    conv_out, conv_new = short_conv(cb, cc, cx, conv_w, conv_state)

    mq = mq.reshape(B, T, MOBA_HEADS, HEAD_DIM)
    mk = mk.reshape(B, T, MOBA_HEADS, HEAD_DIM)
    mv = mv.reshape(B, T, MOBA_HEADS, HEAD_DIM)
    mk_all = jnp.concatenate([mk_past.astype(mk.dtype), mk], axis=1)
    mv_all = jnp.concatenate([mv_past.astype(mv.dtype), mv], axis=1)
    moba_out = moba_attend(mq, q_pos, mk_all, mv_all)

    dq = dq.reshape(B, T, DIFF_HEADS, 2, DIFF_QK_DIM)
    dk = dk.reshape(B, T, DIFF_HEADS, 2 * DIFF_QK_DIM)
    dv = dv.reshape(B, T, DIFF_HEADS, DIFF_V_DIM)
    dk_all = jnp.concatenate([dk_past.astype(dk.dtype), dk], axis=1).reshape(B, -1, DIFF_HEADS, 2, DIFF_QK_DIM)
    dv_all = jnp.concatenate([dv_past.astype(dv.dtype), dv], axis=1)
    lam = (jnp.exp(jnp.sum(lq1.astype(jnp.float32) * lk1.astype(jnp.float32)))
           - jnp.exp(jnp.sum(lq2.astype(jnp.float32) * lk2.astype(jnp.float32))) + lam_init)
    diff_out = diff_attend(dq, q_pos, dk_all, dv_all, lam)
    diff_out = rmsnorm(diff_out, subln_g) * (1.0 - lam_init)

    merged = jnp.concatenate([conv_out, moba_out.reshape(B, T, -1), diff_out.reshape(B, T, -1)], axis=-1)
    return merged @ w_out, (mk, mv, dk, dv, conv_new)


def peer_ffn(h, w_query, sub_keys, expert_u, expert_v):
    B, T, D = h.shape
    n = B * T
    c = min(PEER_TOKEN_BLOCK, n)
    nb = -(-n // c)
    tok = jnp.pad(h.reshape(n, D), ((0, nb * c - n), (0, 0))).reshape(nb, c, D)

    def block_fn(hc):
        q = (hc @ w_query).reshape(c, PEER_HEADS, 2, PEER_HALF).astype(jnp.float32)
        s = jnp.einsum('thpd,hpkd->thpk', q, sub_keys.astype(jnp.float32))
        s_top, i_top = lax.top_k(s, PEER_TOPK)
        cand = s_top[:, :, 0, :, None] + s_top[:, :, 1, None, :]
        cand_idx = i_top[:, :, 0, :, None] * PEER_N_KEYS + i_top[:, :, 1, None, :]
        best, pos = lax.top_k(cand.reshape(c, PEER_HEADS, -1), PEER_TOPK)
        idx = jnp.take_along_axis(cand_idx.reshape(c, PEER_HEADS, -1), pos, axis=-1)
        g = jax.nn.softmax(best, axis=-1)
        u = expert_u[idx].astype(jnp.float32)
        a = jax.nn.gelu(jnp.einsum('td,thkd->thk', hc.astype(jnp.float32), u), approximate=False)
        out = jnp.einsum('thk,thkd->td', g * a, expert_v[idx].astype(jnp.float32))
        return out.astype(hc.dtype)

    out = lax.map(block_fn, tok)
    return out.reshape(nb * c, D)[:n].reshape(B, T, D)


def trunk_layer(x, mk_past, mv_past, dk_past, dv_past, conv_state, g_mix, w_in, conv_w, lq1, lk1, lq2, lk2, lam_init,
                subln_g, w_out, g_ffn, peer_wq, peer_keys, peer_u, peer_v):
    y, new_state = token_mix(rmsnorm(x, g_mix), mk_past, mv_past, dk_past, dv_past, conv_state,
                             w_in, conv_w, lq1, lk1, lq2, lk2, lam_init, subln_g, w_out)
    x = x + y
    x = x + peer_ffn(rmsnorm(x, g_ffn), peer_wq, peer_keys, peer_u, peer_v)
    return x, new_state


def setup_inputs(seed: int = 0) -> dict:
    key = jax.random.key(seed)
    ks = jax.random.split(key, 24)
    n_pages = PAST_LEN // PAGE_SIZE
    n_used = DEC_BATCH * n_pages
    n_pool = n_used + max(1, n_used // 4)

    def nrm(k, shape, s):
        return jax.random.normal(k, shape, jnp.float32) * s

    page_table = jax.random.permutation(ks[7], n_pool)[:n_used].reshape(DEC_BATCH, n_pages).astype(jnp.int32)
    return {
        'x_prompt': nrm(ks[0], (BATCH, SEQ, D_MODEL), 1.0),
        'x_sample': nrm(ks[1], (DEC_BATCH, DEC_SEQ, D_MODEL), 1.0),
        'cache_moba_k': nrm(ks[2], (DEPTH, n_pool, PAGE_SIZE, MOBA_HEADS, HEAD_DIM), 1.0),
        'cache_moba_v': nrm(ks[3], (DEPTH, n_pool, PAGE_SIZE, MOBA_HEADS, HEAD_DIM), 1.0),
        'cache_diff_k': nrm(ks[4], (DEPTH, n_pool, PAGE_SIZE, DIFF_HEADS, 2 * DIFF_QK_DIM), 1.0),
        'cache_diff_v': nrm(ks[5], (DEPTH, n_pool, PAGE_SIZE, DIFF_HEADS, DIFF_V_DIM), 1.0),
        'state_conv': nrm(ks[6], (DEPTH, DEC_BATCH, CONV_WIDTH - 1, CONV_CH), 1.0),
        'page_table': page_table,
        'g_mix': 1.0 + nrm(ks[8], (DEPTH, D_MODEL), 0.01),
        'w_in': nrm(ks[9], (DEPTH, D_MODEL, IN_WIDTH), D_MODEL ** -0.5),
        'conv_w': nrm(ks[10], (DEPTH, CONV_WIDTH, CONV_CH), CONV_WIDTH ** -0.5),
        'diff_lambda_q1': nrm(ks[11], (DEPTH, DIFF_QK_DIM), 0.1),
        'diff_lambda_k1': nrm(ks[12], (DEPTH, DIFF_QK_DIM), 0.1),
        'diff_lambda_q2': nrm(ks[13], (DEPTH, DIFF_QK_DIM), 0.1),
        'diff_lambda_k2': nrm(ks[14], (DEPTH, DIFF_QK_DIM), 0.1),
        'diff_subln_g': 1.0 + nrm(ks[15], (DEPTH, DIFF_V_DIM), 0.01),
        'w_out': nrm(ks[16], (DEPTH, MIX_WIDTH, D_MODEL), MIX_WIDTH ** -0.5),
        'g_ffn': 1.0 + nrm(ks[17], (DEPTH, D_MODEL), 0.01),
        'peer_w_query': nrm(ks[18], (DEPTH, D_MODEL, PEER_HEADS * PEER_QDIM), D_MODEL ** -0.5),
        'peer_sub_keys': nrm(ks[19], (DEPTH, PEER_HEADS, 2, PEER_N_KEYS, PEER_HALF), PEER_HALF ** -0.5),
        'peer_expert_u': nrm(ks[20], (DEPTH, PEER_EXPERTS, D_MODEL), D_MODEL ** -0.5),
        'peer_expert_v': nrm(ks[21], (DEPTH, PEER_EXPERTS, D_MODEL), (PEER_HEADS * PEER_TOPK) ** -0.5),
        'g_final': 1.0 + nrm(ks[22], (D_MODEL,), 0.01),
    }


def reference(x_prompt, x_sample, cache_moba_k, cache_moba_v, cache_diff_k, cache_diff_v, state_conv, page_table,
              g_mix, w_in, conv_w, diff_lambda_q1, diff_lambda_k1, diff_lambda_q2, diff_lambda_k2, diff_subln_g,
              w_out, g_ffn, peer_w_query, peer_sub_keys, peer_expert_u, peer_expert_v, g_final):
    B = x_prompt.shape[0]
    Bs = x_sample.shape[0]
    dt = x_prompt.dtype
    xp, xs = x_prompt, x_sample
    prompt_new, sample_new = [], []
    for l in range(DEPTH):
        lam_init = 0.8 - 0.6 * math.exp(-0.3 * l)
        w = (g_mix[l], w_in[l], conv_w[l], diff_lambda_q1[l], diff_lambda_k1[l], diff_lambda_q2[l], diff_lambda_k2[l],
             lam_init, diff_subln_g[l], w_out[l], g_ffn[l], peer_w_query[l], peer_sub_keys[l], peer_expert_u[l],
             peer_expert_v[l])
        empty_m = jnp.zeros((B, 0, MOBA_HEADS, HEAD_DIM), dt)
        empty_dk = jnp.zeros((B, 0, DIFF_HEADS, 2 * DIFF_QK_DIM), dt)
        empty_dv = jnp.zeros((B, 0, DIFF_HEADS, DIFF_V_DIM), dt)
        zero_conv = jnp.zeros((B, CONV_WIDTH - 1, CONV_CH), dt)
        xp, st = trunk_layer(xp, empty_m, empty_m, empty_dk, empty_dv, zero_conv, *w)
        prompt_new.append(st)
        mk_past = cache_moba_k[l][page_table].reshape((Bs, -1) + cache_moba_k.shape[3:])
        mv_past = cache_moba_v[l][page_table].reshape((Bs, -1) + cache_moba_v.shape[3:])
        dk_past = cache_diff_k[l][page_table].reshape((Bs, -1) + cache_diff_k.shape[3:])
        dv_past = cache_diff_v[l][page_table].reshape((Bs, -1) + cache_diff_v.shape[3:])
        xs, st = trunk_layer(xs, mk_past, mv_past, dk_past, dv_past, state_conv[l], *w)
        sample_new.append(st)
    y_prompt = rmsnorm(xp, g_final)
    y_sample = rmsnorm(xs, g_final)
    p_moba_k, p_moba_v, p_diff_k, p_diff_v, p_conv = [jnp.stack(s) for s in zip(*prompt_new)]
    s_moba_k, s_moba_v, s_diff_k, s_diff_v, s_conv = [jnp.stack(s) for s in zip(*sample_new)]
    return (y_prompt, y_sample, p_moba_k, p_moba_v, p_diff_k, p_diff_v, p_conv,
            s_moba_k, s_moba_v, s_diff_k, s_diff_v, s_conv)
```

```python
import functools
import math

import jax
import jax.numpy as jnp
from jax import lax
from jax.experimental import pallas as pl
from jax.experimental.pallas import tpu as pltpu

D_MODEL = 2048
BATCH = 4
SEQ = 2048
DEPTH = 2
DEC_BATCH = 32
DEC_SEQ = 4
PAST_LEN = 8192
PAGE_SIZE = 128
HEAD_DIM = 128
CONV_CH = 512
CONV_WIDTH = 3
MOBA_HEADS = 6
MOBA_BLOCK = 256
MOBA_TOPK = 3
DIFF_HEADS = 6
DIFF_QK_DIM = 64
PEER_HEADS = 8
PEER_N_KEYS = 128
PEER_EXPERTS = PEER_N_KEYS * PEER_N_KEYS
PEER_TOPK = 16
EPS = 1e-6
NEG = -1e30

N_PAGES = PAST_LEN // PAGE_SIZE
N_SAMPLE = DEC_BATCH * DEC_SEQ
N_PROMPT = BATCH * SEQ
N_PROMPT_BLOCKS = SEQ // MOBA_BLOCK
N_PAST_BLOCKS = PAST_LEN // MOBA_BLOCK

COL_MQ, COL_MK, COL_MV = 12, 18, 24
COL_DQ, COL_DK, COL_DV = 30, 36, 42

LANES = 128
SUBLANES = 8
MIB = 1024 * 1024

PAGES_PER_STEP = 16
N_PAGE_STEPS = N_PAGES // PAGES_PER_STEP
KEYS_PER_STEP = PAGES_PER_STEP * PAGE_SIZE
BLOCKS_PER_STEP = KEYS_PER_STEP // MOBA_BLOCK

PEER_EXPERT_CHUNK = 1024
W_PITCH = PEER_N_KEYS + SUBLANES

F32 = jnp.float32
BF16 = jnp.bfloat16
HI = lax.Precision.HIGHEST
NT = (((1,), (1,)), ((), ()))


def _cparams(sem, vmem_mib):
    return pltpu.CompilerParams(dimension_semantics=sem, vmem_limit_bytes=vmem_mib * MIB)


def _proj_kernel(x_ref, g_ref, *rest, norm, split, emit_h):
    if split:
        w_hi_ref, w_lo_ref = rest[0], rest[1]
        rest = rest[2:]
    else:
        w_hi_ref = rest[0]
        rest = rest[1:]
    o_ref = rest[0]
    rest = rest[1:]
    if emit_h:
        hout_ref = rest[0]
        rest = rest[1:]
    h_hi = rest[0]
    h_lo = rest[1] if split else None

    @pl.when(pl.program_id(1) == 0)
    def _():
        x = x_ref[...]
        if norm:
            ms = jnp.mean(x * x, axis=-1, keepdims=True)
            x = x * lax.rsqrt(ms + EPS) * g_ref[...]
        hb = x.astype(BF16)
        h_hi[...] = hb
        if split:
            h_lo[...] = (x - hb.astype(F32)).astype(BF16)
        if emit_h:
            hout_ref[...] = hb

    acc = jnp.dot(h_hi[...], w_hi_ref[...], preferred_element_type=F32)
    if split:
        acc += jnp.dot(h_hi[...], w_lo_ref[...], preferred_element_type=F32)
        acc += jnp.dot(h_lo[...], w_hi_ref[...], preferred_element_type=F32)
    o_ref[...] = acc


def _project(x, g, w_hi, w_lo=None, *, norm=True, emit_h=False, tm, tn):
    t, d = x.shape
    n = w_hi.shape[1]
    split = w_lo is not None
    in_specs = [pl.BlockSpec((tm, d), lambda i, j: (i, 0)),
                pl.BlockSpec((1, d), lambda i, j: (0, 0)),
                pl.BlockSpec((d, tn), lambda i, j: (0, j))]
    args = [x, g.reshape(1, d), w_hi]
    if split:
        in_specs.append(pl.BlockSpec((d, tn), lambda i, j: (0, j)))
        args.append(w_lo)
    out_shape = [jax.ShapeDtypeStruct((t, n), F32)]
    out_specs = [pl.BlockSpec((tm, tn), lambda i, j: (i, j))]
    if emit_h:
        out_shape.append(jax.ShapeDtypeStruct((t, d), BF16))
        out_specs.append(pl.BlockSpec((tm, d), lambda i, j: (i, 0)))
    scratch = [pltpu.VMEM((tm, d), BF16)]
    if split:
        scratch.append(pltpu.VMEM((tm, d), BF16))
    res = pl.pallas_call(
        functools.partial(_proj_kernel, norm=norm, split=split, emit_h=emit_h),
        grid=(t // tm, n // tn),
        in_specs=in_specs, out_specs=out_specs, out_shape=out_shape,
        scratch_shapes=scratch,
        compiler_params=_cparams(("parallel", "arbitrary"), 48),
        name="proj_split" if split else "proj",
    )(*args)
    return res if emit_h else res[0]


def _conv_prompt_kernel(cb_ref, cc_ref, cx_ref, w_ref, y_ref, st_ref, u_scr):
    t = cb_ref.shape[0]
    u = cc_ref[...] * cx_ref[...]
    u_scr[0:SUBLANES, :] = jnp.zeros((SUBLANES, CONV_CH), F32)
    u_scr[SUBLANES:SUBLANES + t, :] = u
    w = w_ref[...]
    y = w[0:1, :] * u_scr[SUBLANES - 2:SUBLANES - 2 + t, :]
    y = y + w[1:2, :] * u_scr[SUBLANES - 1:SUBLANES - 1 + t, :]
    y = y + w[2:3, :] * u
    y_ref[...] = cb_ref[...] * y
    st_ref[...] = u_scr[SUBLANES + t - 2:SUBLANES + t, :]


def _conv_prompt(proj, conv_w):
    return pl.pallas_call(
        _conv_prompt_kernel,
        grid=(BATCH,),
        in_specs=[pl.BlockSpec((SEQ, CONV_CH), lambda b: (b, 0)),
                  pl.BlockSpec((SEQ, CONV_CH), lambda b: (b, 1)),
                  pl.BlockSpec((SEQ, CONV_CH), lambda b: (b, 2)),
                  pl.BlockSpec((CONV_WIDTH, CONV_CH), lambda b: (0, 0))],
        out_specs=[pl.BlockSpec((SEQ, CONV_CH), lambda b: (b, 0)),
                   pl.BlockSpec((None, CONV_WIDTH - 1, CONV_CH), lambda b: (b, 0, 0))],
        out_shape=[jax.ShapeDtypeStruct((N_PROMPT, CONV_CH), F32),
                   jax.ShapeDtypeStruct((BATCH, CONV_WIDTH - 1, CONV_CH), F32)],
        scratch_shapes=[pltpu.VMEM((SEQ + SUBLANES, CONV_CH), F32)],
        compiler_params=_cparams(("parallel",), 48),
        name="conv_prompt",
    )(proj, proj, proj, conv_w)


def _conv_sample_kernel(cb_ref, cc_ref, cx_ref, w_ref, st_in_ref, y_ref, st_ref, u_scr):
    w = w_ref[...]
    for s in range(DEC_BATCH):
        r = s * DEC_SEQ
        u_scr[0:2, :] = st_in_ref[s]
        u_scr[2:2 + DEC_SEQ, :] = cc_ref[r:r + DEC_SEQ, :] * cx_ref[r:r + DEC_SEQ, :]
        y = w[0:1, :] * u_scr[0:DEC_SEQ, :]
        y = y + w[1:2, :] * u_scr[1:1 + DEC_SEQ, :]
        y = y + w[2:3, :] * u_scr[2:2 + DEC_SEQ, :]
        y_ref[r:r + DEC_SEQ, :] = cb_ref[r:r + DEC_SEQ, :] * y
        st_ref[s] = u_scr[DEC_SEQ:DEC_SEQ + 2, :]


def _conv_sample(proj, conv_w, state):
    return pl.pallas_call(
        _conv_sample_kernel,
        grid=(1,),
        in_specs=[pl.BlockSpec((N_SAMPLE, CONV_CH), lambda i: (0, 0)),
                  pl.BlockSpec((N_SAMPLE, CONV_CH), lambda i: (0, 1)),
                  pl.BlockSpec((N_SAMPLE, CONV_CH), lambda i: (0, 2)),
                  pl.BlockSpec((CONV_WIDTH, CONV_CH), lambda i: (0, 0)),
                  pl.BlockSpec((DEC_BATCH, CONV_WIDTH - 1, CONV_CH), lambda i: (0, 0, 0))],
        out_specs=[pl.BlockSpec((N_SAMPLE, CONV_CH), lambda i: (0, 0)),
                   pl.BlockSpec((DEC_BATCH, CONV_WIDTH - 1, CONV_CH), lambda i: (0, 0, 0))],
        out_shape=[jax.ShapeDtypeStruct((N_SAMPLE, CONV_CH), F32),
                   jax.ShapeDtypeStruct((DEC_BATCH, CONV_WIDTH - 1, CONV_CH), F32)],
        scratch_shapes=[pltpu.VMEM((SUBLANES, CONV_CH), F32)],
        name="conv_sample",
    )(proj, proj, proj, conv_w, state)


def _rank_lt(gate, n_cols, k):
    col_id = lax.broadcasted_iota(jnp.int32, gate.shape, 1)
    rank = jnp.zeros(gate.shape, jnp.int32)
    for n2 in range(n_cols):
        other = gate[:, n2:n2 + 1]
        beats = jnp.where(other > gate, 1, jnp.where(other == gate, jnp.where(col_id > n2, 1, 0), 0))
        rank = rank + beats
    return jnp.where(rank < k, 1, 0)


def _moba_prompt_kernel(q_ref, k_ref, v_ref, o_ref, kb_scr, vb_scr, km_scr, m_scr, l_scr, acc_scr):
    i = pl.program_id(2)
    tq = MOBA_BLOCK
    scale = HEAD_DIM ** -0.5

    @pl.when(i == 0)
    def _():
        k = k_ref[...]
        kb_scr[...] = k.astype(BF16)
        vb_scr[...] = v_ref[...].astype(BF16)
        km_scr[...] = jnp.sum(k.reshape(N_PROMPT_BLOCKS, MOBA_BLOCK, HEAD_DIM), axis=1) * (1.0 / MOBA_BLOCK)

    q = q_ref[...]
    gate = lax.dot_general(q, km_scr[...], NT, precision=HI, preferred_element_type=F32)
    blk = lax.broadcasted_iota(jnp.int32, gate.shape, 1)
    gate = jnp.where(blk < i, gate, NEG)
    sel = _rank_lt(gate, N_PROMPT_BLOCKS, MOBA_TOPK) * jnp.where(blk < i, 1, 0)
    qb = q.astype(BF16)

    own = pl.multiple_of(i * tq, tq)
    s = lax.dot_general(qb, kb_scr[pl.ds(own, tq), :], NT, preferred_element_type=F32) * scale
    row = lax.broadcasted_iota(jnp.int32, (tq, tq), 0)
    col = lax.broadcasted_iota(jnp.int32, (tq, tq), 1)
    s = jnp.where(col <= row, s, NEG)
    m = jnp.max(s, axis=-1, keepdims=True)
    p = jnp.exp(s - m)
    m_scr[...] = m
    l_scr[...] = jnp.sum(p, axis=-1, keepdims=True)
    acc_scr[...] = jnp.dot(p.astype(BF16), vb_scr[pl.ds(own, tq), :], preferred_element_type=F32)

    for n in range(N_PROMPT_BLOCKS - 1):
        @pl.when(n < i)
        def _(n=n):
            sn = lax.dot_general(qb, kb_scr[n * tq:(n + 1) * tq, :], NT, preferred_element_type=F32) * scale
            sn = jnp.where(sel[:, n:n + 1] > 0, sn, NEG)
            m_old = m_scr[...]
            m_new = jnp.maximum(m_old, jnp.max(sn, axis=-1, keepdims=True))
            alpha = jnp.exp(m_old - m_new)
            pn = jnp.exp(sn - m_new)
            l_scr[...] = alpha * l_scr[...] + jnp.sum(pn, axis=-1, keepdims=True)
            acc_scr[...] = alpha * acc_scr[...] + jnp.dot(pn.astype(BF16), vb_scr[n * tq:(n + 1) * tq, :],
                                                          preferred_element_type=F32)
            m_scr[...] = m_new

    o_ref[...] = acc_scr[...] / l_scr[...]


def _moba_prompt(proj):
    tq = MOBA_BLOCK
    return pl.pallas_call(
        _moba_prompt_kernel,
        grid=(BATCH, MOBA_HEADS, N_PROMPT_BLOCKS),
        in_specs=[pl.BlockSpec((tq, HEAD_DIM), lambda b, h, i: (b * N_PROMPT_BLOCKS + i, COL_MQ + h)),
                  pl.BlockSpec((SEQ, HEAD_DIM), lambda b, h, i: (b, COL_MK + h)),
                  pl.BlockSpec((SEQ, HEAD_DIM), lambda b, h, i: (b, COL_MV + h))],
        out_specs=pl.BlockSpec((tq, HEAD_DIM), lambda b, h, i: (b * N_PROMPT_BLOCKS + i, h)),
        out_shape=jax.ShapeDtypeStruct((N_PROMPT, MOBA_HEADS * HEAD_DIM), F32),
        scratch_shapes=[pltpu.VMEM((SEQ, HEAD_DIM), BF16), pltpu.VMEM((SEQ, HEAD_DIM), BF16),
                        pltpu.VMEM((N_PROMPT_BLOCKS, HEAD_DIM), F32),
                        pltpu.VMEM((tq, 1), F32), pltpu.VMEM((tq, 1), F32), pltpu.VMEM((tq, HEAD_DIM), F32)],
        compiler_params=_cparams(("parallel", "parallel", "arbitrary"), 32),
        name="moba_prompt",
    )(proj, proj, proj)


def _diff_lambda(lq1_ref, lk1_ref, lq2_ref, lk2_ref, lam_init):
    a = jnp.sum(lq1_ref[...] * lk1_ref[...], axis=-1, keepdims=True)
    b = jnp.sum(lq2_ref[...] * lk2_ref[...], axis=-1, keepdims=True)
    return jnp.exp(a) - jnp.exp(b) + lam_init


def _diff_finish(o1, o2, lam, g, lam_init):
    o = o1 - lam * o2
    ms = jnp.mean(o * o, axis=-1, keepdims=True)
    return o * lax.rsqrt(ms + EPS) * g * (1.0 - lam_init)


def _split_maps(q):
    lane = lax.broadcasted_iota(jnp.int32, q.shape, 1)
    return jnp.where(lane < DIFF_QK_DIM, q, 0.0), jnp.where(lane >= DIFF_QK_DIM, q, 0.0)


def _diff_prompt_kernel(lq1_ref, lk1_ref, lq2_ref, lk2_ref, g_ref, q_ref, k_ref, v_ref, o_ref,
                        kb_scr, vb_scr, m_scr, l_scr, acc_scr, *, lam_init):
    i = pl.program_id(2)
    tq = MOBA_BLOCK
    scale = DIFF_QK_DIM ** -0.5

    @pl.when(i == 0)
    def _():
        kb_scr[...] = k_ref[...].astype(BF16)
        vb_scr[...] = v_ref[...].astype(BF16)

    q1, q2 = _split_maps(q_ref[...])
    qs = (q1.astype(BF16), q2.astype(BF16))
    own = pl.multiple_of(i * tq, tq)
    row = lax.broadcasted_iota(jnp.int32, (tq, tq), 0)
    col = lax.broadcasted_iota(jnp.int32, (tq, tq), 1)
    for c in range(2):
        s = lax.dot_general(qs[c], kb_scr[pl.ds(own, tq), :], NT, preferred_element_type=F32) * scale
        s = jnp.where(col <= row, s, NEG)
        m = jnp.max(s, axis=-1, keepdims=True)
        p = jnp.exp(s - m)
        m_scr[c] = m
        l_scr[c] = jnp.sum(p, axis=-1, keepdims=True)
        acc_scr[c] = jnp.dot(p.astype(BF16), vb_scr[pl.ds(own, tq), :], preferred_element_type=F32)

    for n in range(N_PROMPT_BLOCKS - 1):
        @pl.when(n < i)
        def _(n=n):
            kb = kb_scr[n * tq:(n + 1) * tq, :]
            vb = vb_scr[n * tq:(n + 1) * tq, :]
            for c in range(2):
                sn = lax.dot_general(qs[c], kb, NT, preferred_element_type=F32) * scale
                m_old = m_scr[c]
                m_new = jnp.maximum(m_old, jnp.max(sn, axis=-1, keepdims=True))
                alpha = jnp.exp(m_old - m_new)
                pn = jnp.exp(sn - m_new)
                l_scr[c] = alpha * l_scr[c] + jnp.sum(pn, axis=-1, keepdims=True)
                acc_scr[c] = alpha * acc_scr[c] + jnp.dot(pn.astype(BF16), vb, preferred_element_type=F32)
                m_scr[c] = m_new

    lam = _diff_lambda(lq1_ref, lk1_ref, lq2_ref, lk2_ref, lam_init)
    o_ref[...] = _diff_finish(acc_scr[0] / l_scr[0], acc_scr[1] / l_scr[1], lam, g_ref[...], lam_init)


def _diff_prompt(proj, lams, subln_g, lam_init):
    tq = MOBA_BLOCK
    lam_specs = [pl.BlockSpec((1, DIFF_QK_DIM), lambda b, h, i: (0, 0))] * 4
    return pl.pallas_call(
        functools.partial(_diff_prompt_kernel, lam_init=lam_init),
        grid=(BATCH, DIFF_HEADS, N_PROMPT_BLOCKS),
        in_specs=lam_specs + [
            pl.BlockSpec((1, HEAD_DIM), lambda b, h, i: (0, 0)),
            pl.BlockSpec((tq, HEAD_DIM), lambda b, h, i: (b * N_PROMPT_BLOCKS + i, COL_DQ + h)),
            pl.BlockSpec((SEQ, HEAD_DIM), lambda b, h, i: (b, COL_DK + h)),
            pl.BlockSpec((SEQ, HEAD_DIM), lambda b, h, i: (b, COL_DV + h))],
        out_specs=pl.BlockSpec((tq, HEAD_DIM), lambda b, h, i: (b * N_PROMPT_BLOCKS + i, h)),
        out_shape=jax.ShapeDtypeStruct((N_PROMPT, DIFF_HEADS * HEAD_DIM), F32),
        scratch_shapes=[pltpu.VMEM((SEQ, HEAD_DIM), BF16), pltpu.VMEM((SEQ, HEAD_DIM), BF16),
                        pltpu.VMEM((2, tq, 1), F32), pltpu.VMEM((2, tq, 1), F32),
                        pltpu.VMEM((2, tq, HEAD_DIM), F32)],
        compiler_params=_cparams(("parallel", "parallel", "arbitrary"), 32),
        name="diff_prompt",
    )(*lams, subln_g.reshape(1, HEAD_DIM), proj, proj, proj)


def _page_specs(layer, pinned_phase=None):
    specs = []
    for r in range(PAGES_PER_STEP):
        if pinned_phase is None:
            def imap(b, s, pt, r=r):
                return (layer, pt[b, s * PAGES_PER_STEP + r], 0, 0, 0)
        elif pinned_phase == 1:
            def imap(b, ph, s, pt, r=r):
                st = jnp.where(ph == 0, s, N_PAGE_STEPS - 1)
                return (layer, pt[b, st * PAGES_PER_STEP + r], 0, 0, 0)
        else:
            def imap(b, ph, s, pt, r=r):
                st = jnp.where(ph == 0, 0, s)
                return (layer, pt[b, st * PAGES_PER_STEP + r], 0, 0, 0)
        specs.append(pl.BlockSpec((None, None, PAGE_SIZE, MOBA_HEADS, HEAD_DIM), imap))
    return specs


def _load_pages(refs, h):
    return jnp.concatenate([r[:, h, :] for r in refs], axis=0)


def _new_key_mask():
    r = lax.broadcasted_iota(jnp.int32, (SUBLANES, PAGE_SIZE), 0)
    j = lax.broadcasted_iota(jnp.int32, (SUBLANES, PAGE_SIZE), 1)
    return (j <= (r & (DEC_SEQ - 1))) & (j < DEC_SEQ)


def _stage_new_kv(kn_ref, vn_ref, knp_scr, vnp_scr, h):
    knp_scr[...] = jnp.zeros(knp_scr.shape, F32)
    vnp_scr[...] = jnp.zeros(vnp_scr.shape, F32)
    knp_scr[0:DEC_SEQ, :] = kn_ref[:, h * HEAD_DIM:(h + 1) * HEAD_DIM]
    vnp_scr[0:DEC_SEQ, :] = vn_ref[:, h * HEAD_DIM:(h + 1) * HEAD_DIM]


def _diff_sample_kernel(pt_ref, lq1_ref, lk1_ref, lq2_ref, lk2_ref, g_ref, q_ref, kn_ref, vn_ref, *rest,
                        lam_init):
    k_refs = rest[:PAGES_PER_STEP]
    v_refs = rest[PAGES_PER_STEP:2 * PAGES_PER_STEP]
    o_ref = rest[2 * PAGES_PER_STEP]
    q8_scr, knp_scr, vnp_scr, m_scr, l_scr, acc_scr = rest[2 * PAGES_PER_STEP + 1:]
    step = pl.program_id(1)
    scale = DIFF_QK_DIM ** -0.5

    @pl.when(step == 0)
    def _():
        for h in range(DIFF_HEADS):
            q1, q2 = _split_maps(q_ref[:, h * HEAD_DIM:(h + 1) * HEAD_DIM])
            q8_scr[h, 0:DEC_SEQ, :] = q1
            q8_scr[h, DEC_SEQ:2 * DEC_SEQ, :] = q2
        m_scr[...] = jnp.full(m_scr.shape, -jnp.inf, F32)
        l_scr[...] = jnp.zeros(l_scr.shape, F32)
        acc_scr[...] = jnp.zeros(acc_scr.shape, F32)

    def update(h, s, vb):
        m_old = m_scr[h]
        m_new = jnp.maximum(m_old, jnp.max(s, axis=-1, keepdims=True))
        alpha = jnp.exp(m_old - m_new)
        p = jnp.exp(s - m_new)
        l_scr[h] = alpha * l_scr[h] + jnp.sum(p, axis=-1, keepdims=True)
        acc_scr[h] = alpha * acc_scr[h] + jnp.dot(p.astype(BF16), vb, preferred_element_type=F32)
        m_scr[h] = m_new

    for h in range(DIFF_HEADS):
        q8 = q8_scr[h].astype(BF16)
        kb = _load_pages(k_refs, h).astype(BF16)
        vb = _load_pages(v_refs, h).astype(BF16)
        update(h, lax.dot_general(q8, kb, NT, preferred_element_type=F32) * scale, vb)

    @pl.when(step == N_PAGE_STEPS - 1)
    def _():
        lam = _diff_lambda(lq1_ref, lk1_ref, lq2_ref, lk2_ref, lam_init)
        for h in range(DIFF_HEADS):
            q8 = q8_scr[h].astype(BF16)
            _stage_new_kv(kn_ref, vn_ref, knp_scr, vnp_scr, h)
            sn = lax.dot_general(q8, knp_scr[...].astype(BF16), NT, preferred_element_type=F32) * scale
            update(h, jnp.where(_new_key_mask(), sn, NEG), vnp_scr[...].astype(BF16))
            o = acc_scr[h] / l_scr[h]
            o_ref[:, h * HEAD_DIM:(h + 1) * HEAD_DIM] = _diff_finish(
                o[0:DEC_SEQ, :], o[DEC_SEQ:2 * DEC_SEQ, :], lam, g_ref[...], lam_init)


def _tok_spec(c0):
    width = MOBA_HEADS * HEAD_DIM
    return pl.BlockSpec((None, DEC_SEQ, width), lambda b, *_: (b, 0, c0 // MOBA_HEADS))


def _diff_sample(proj3, cache_k, cache_v, page_table, layer, lams, subln_g, lam_init):
    lam_specs = [pl.BlockSpec((1, DIFF_QK_DIM), lambda b, s, pt: (0, 0))] * 4
    grid_spec = pltpu.PrefetchScalarGridSpec(
        num_scalar_prefetch=1,
        grid=(DEC_BATCH, N_PAGE_STEPS),
        in_specs=lam_specs + [pl.BlockSpec((1, HEAD_DIM), lambda b, s, pt: (0, 0)),
                              _tok_spec(COL_DQ), _tok_spec(COL_DK), _tok_spec(COL_DV)]
        + _page_specs(layer) + _page_specs(layer),
        out_specs=pl.BlockSpec((None, DEC_SEQ, DIFF_HEADS * HEAD_DIM), lambda b, s, pt: (b, 0, 0)),
        scratch_shapes=[pltpu.VMEM((DIFF_HEADS, SUBLANES, HEAD_DIM), F32),
                        pltpu.VMEM((PAGE_SIZE, HEAD_DIM), F32), pltpu.VMEM((PAGE_SIZE, HEAD_DIM), F32),
                        pltpu.VMEM((DIFF_HEADS, SUBLANES, 1), F32), pltpu.VMEM((DIFF_HEADS, SUBLANES, 1), F32),
                        pltpu.VMEM((DIFF_HEADS, SUBLANES, HEAD_DIM), F32)])
    return pl.pallas_call(
        functools.partial(_diff_sample_kernel, lam_init=lam_init),
        grid_spec=grid_spec,
        out_shape=jax.ShapeDtypeStruct((DEC_BATCH, DEC_SEQ, DIFF_HEADS * HEAD_DIM), F32),
        compiler_params=_cparams(("parallel", "arbitrary"), 56),
        name="diff_sample",
    )(page_table, *lams, subln_g.reshape(1, HEAD_DIM), proj3, proj3, proj3,
      *([cache_k] * PAGES_PER_STEP), *([cache_v] * PAGES_PER_STEP))


def _moba_sample_kernel(pt_ref, q_ref, kn_ref, vn_ref, *rest):
    k_refs = rest[:PAGES_PER_STEP]
    v_refs = rest[PAGES_PER_STEP:2 * PAGES_PER_STEP]
    o_ref = rest[2 * PAGES_PER_STEP]
    q8_scr, knp_scr, vnp_scr, s_scr, km_scr, p_scr, l_scr, acc_scr = rest[2 * PAGES_PER_STEP + 1:]
    phase = pl.program_id(1)
    step = pl.program_id(2)
    scale = HEAD_DIM ** -0.5

    @pl.when((phase == 0) & (step == 0))
    def _():
        q8_scr[...] = jnp.zeros(q8_scr.shape, F32)
        for h in range(MOBA_HEADS):
            q8_scr[h, 0:DEC_SEQ, :] = q_ref[:, h * HEAD_DIM:(h + 1) * HEAD_DIM]

    @pl.when(phase == 0)
    def _():
        for h in range(MOBA_HEADS):
            k = _load_pages(k_refs, h)
            km_scr[h, step] = (jnp.sum(k.reshape(BLOCKS_PER_STEP, MOBA_BLOCK, HEAD_DIM), axis=1)
                               * (1.0 / MOBA_BLOCK))
            s_scr[h, step] = lax.dot_general(q8_scr[h].astype(BF16), k.astype(BF16), NT,
                                             preferred_element_type=F32) * scale

    @pl.when((phase == 1) & (step == 0))
    def _():
        for h in range(MOBA_HEADS):
            q8 = q8_scr[h]
            km = km_scr[h].reshape(N_PAST_BLOCKS, HEAD_DIM)
            gate = lax.dot_general(q8, km, NT, precision=HI, preferred_element_type=F32)
            sel = _rank_lt(gate, N_PAST_BLOCKS, MOBA_TOPK)
            _stage_new_kv(kn_ref, vn_ref, knp_scr, vnp_scr, h)
            sn = lax.dot_general(q8.astype(BF16), knp_scr[...].astype(BF16), NT,
                                 preferred_element_type=F32) * scale
            sn = jnp.where(_new_key_mask(), sn, NEG)
            m = jnp.max(sn, axis=-1, keepdims=True)

            def block(n, h=h, sel=sel):
                st, jb = divmod(n, BLOCKS_PER_STEP)
                return jnp.where(sel[:, n:n + 1] > 0,
                                 s_scr[h, st, :, jb * MOBA_BLOCK:(jb + 1) * MOBA_BLOCK], NEG)

            for n in range(N_PAST_BLOCKS):
                m = jnp.maximum(m, jnp.max(block(n), axis=-1, keepdims=True))
            pn = jnp.exp(sn - m)
            l = jnp.sum(pn, axis=-1, keepdims=True)
            for n in range(N_PAST_BLOCKS):
                st, jb = divmod(n, BLOCKS_PER_STEP)
                pb = jnp.exp(block(n) - m)
                l = l + jnp.sum(pb, axis=-1, keepdims=True)
                p_scr[h, st, :, jb * MOBA_BLOCK:(jb + 1) * MOBA_BLOCK] = pb
            l_scr[h] = l
            acc_scr[h] = jnp.dot(pn.astype(BF16), vnp_scr[...].astype(BF16), preferred_element_type=F32)

    @pl.when(phase == 1)
    def _():
        for h in range(MOBA_HEADS):
            vb = _load_pages(v_refs, h).astype(BF16)
            acc_scr[h] += jnp.dot(p_scr[h, step].astype(BF16), vb, preferred_element_type=F32)

    @pl.when((phase == 1) & (step == N_PAGE_STEPS - 1))
    def _():
        for h in range(MOBA_HEADS):
            o = acc_scr[h] / l_scr[h]
            o_ref[:, h * HEAD_DIM:(h + 1) * HEAD_DIM] = o[0:DEC_SEQ, :]


def _moba_sample(proj3, cache_k, cache_v, page_table, layer):
    grid_spec = pltpu.PrefetchScalarGridSpec(
        num_scalar_prefetch=1,
        grid=(DEC_BATCH, 2, N_PAGE_STEPS),
        in_specs=[_tok_spec(COL_MQ), _tok_spec(COL_MK), _tok_spec(COL_MV)]
        + _page_specs(layer, pinned_phase=1) + _page_specs(layer, pinned_phase=0),
        out_specs=pl.BlockSpec((None, DEC_SEQ, MOBA_HEADS * HEAD_DIM), lambda b, ph, s, pt: (b, 0, 0)),
        scratch_shapes=[pltpu.VMEM((MOBA_HEADS, SUBLANES, HEAD_DIM), F32),
                        pltpu.VMEM((PAGE_SIZE, HEAD_DIM), F32), pltpu.VMEM((PAGE_SIZE, HEAD_DIM), F32),
                        pltpu.VMEM((MOBA_HEADS, N_PAGE_STEPS, SUBLANES, KEYS_PER_STEP), F32),
                        pltpu.VMEM((MOBA_HEADS, N_PAGE_STEPS, BLOCKS_PER_STEP, HEAD_DIM), F32),
                        pltpu.VMEM((MOBA_HEADS, N_PAGE_STEPS, SUBLANES, KEYS_PER_STEP), F32),
                        pltpu.VMEM((MOBA_HEADS, SUBLANES, 1), F32),
                        pltpu.VMEM((MOBA_HEADS, SUBLANES, HEAD_DIM), F32)])
    return pl.pallas_call(
        _moba_sample_kernel,
        grid_spec=grid_spec,
        out_shape=jax.ShapeDtypeStruct((DEC_BATCH, DEC_SEQ, MOBA_HEADS * HEAD_DIM), F32),
        compiler_params=_cparams(("parallel", "arbitrary", "arbitrary"), 56),
        name="moba_sample",
    )(page_table, proj3, proj3, proj3, *([cache_k] * PAGES_PER_STEP), *([cache_v] * PAGES_PER_STEP))


def _out_proj_kernel(x_ref, c_ref, m_ref, d_ref, wc_ref, wm_ref, wd_ref, o_ref):
    acc = jnp.dot(c_ref[...].astype(BF16), wc_ref[...], preferred_element_type=F32)
    acc += jnp.dot(m_ref[...].astype(BF16), wm_ref[...], preferred_element_type=F32)
    acc += jnp.dot(d_ref[...].astype(BF16), wd_ref[...], preferred_element_type=F32)
    o_ref[...] = x_ref[...] + acc


def _out_proj(x, conv_o, moba_o, diff_o, wc, wm, wd, tm):
    t = x.shape[0]
    row = lambda w: pl.BlockSpec((tm, w), lambda i: (i, 0))
    full = lambda a: pl.BlockSpec(a.shape, lambda i: (0, 0))
    return pl.pallas_call(
        _out_proj_kernel,
        grid=(t // tm,),
        in_specs=[row(D_MODEL), row(conv_o.shape[1]), row(moba_o.shape[1]), row(diff_o.shape[1]),
                  full(wc), full(wm), full(wd)],
        out_specs=row(D_MODEL),
        out_shape=jax.ShapeDtypeStruct((t, D_MODEL), F32),
        compiler_params=_cparams(("parallel",), 48),
        name="out_proj",
    )(x, conv_o, moba_o, diff_o, wc, wm, wd)


def _top16_rows(s):
    n, t = s.shape
    iota = lax.broadcasted_iota(jnp.int32, (n, t), 0)
    slot = lax.broadcasted_iota(jnp.int32, (PEER_TOPK, t), 0)

    def body(k, carry):
        s, vals, idxs = carry
        m = jnp.max(s, axis=0, keepdims=True)
        idx = jnp.min(jnp.where(s == m, iota, n), axis=0, keepdims=True)
        vals = jnp.where(slot == k, m, vals)
        idxs = jnp.where(slot == k, idx, idxs)
        s = jnp.where(iota == idx, -jnp.inf, s)
        return s, vals, idxs

    _, vals, idxs = lax.fori_loop(0, PEER_TOPK, body,
                                  (s, jnp.zeros((PEER_TOPK, t), F32), jnp.zeros((PEER_TOPK, t), jnp.int32)))
    return vals, idxs


def _route_kernel(q_ref, keys_ref, i1_ref, i2_ref, g_ref):
    i1_parts, i2_parts, g_parts = [], [], []
    for h in range(PEER_HEADS):
        tops = []
        for p in range(2):
            c0 = (h * 2 + p) * PEER_N_KEYS
            q_hp = q_ref[:, c0:c0 + PEER_N_KEYS]
            s_t = lax.dot_general(keys_ref[h, p], q_hp, NT, precision=HI, preferred_element_type=F32)
            tops.append(_top16_rows(s_t))
        (v0, x0), (v1, x1) = tops
        cand = jnp.concatenate([v0[k1:k1 + 1, :] + v1 for k1 in range(PEER_TOPK)], axis=0)
        best, pos = _top16_rows(cand)
        k1 = pos >> 4
        k2 = pos & (PEER_TOPK - 1)
        a = jnp.zeros(pos.shape, jnp.int32)
        b = jnp.zeros(pos.shape, jnp.int32)
        for k in range(PEER_TOPK):
            a = jnp.where(k1 == k, x0[k:k + 1, :], a)
            b = jnp.where(k2 == k, x1[k:k + 1, :], b)
        e = jnp.exp(best - jnp.max(best, axis=0, keepdims=True))
        g_parts.append(e / jnp.sum(e, axis=0, keepdims=True))
        i1_parts.append(a)
        i2_parts.append(b)
    i1_ref[...] = jnp.concatenate(i1_parts, axis=0).T
    i2_ref[...] = jnp.concatenate(i2_parts, axis=0).T
    g_ref[...] = jnp.concatenate(g_parts, axis=0).T


def _route(q, keys):
    t = q.shape[0]
    tb = LANES
    slots = PEER_HEADS * PEER_TOPK
    spec = pl.BlockSpec((tb, slots), lambda i: (i, 0))
    return pl.pallas_call(
        _route_kernel,
        grid=(t // tb,),
        in_specs=[pl.BlockSpec((tb, q.shape[1]), lambda i: (i, 0)),
                  pl.BlockSpec(keys.shape, lambda i: (0, 0, 0, 0))],
        out_specs=[spec, spec, spec],
        out_shape=[jax.ShapeDtypeStruct((t, slots), jnp.int32), jax.ShapeDtypeStruct((t, slots), jnp.int32),
                   jax.ShapeDtypeStruct((t, slots), F32)],
        compiler_params=_cparams(("parallel",), 32),
        name="peer_route",
    )(q, keys)


def _expert_kernel(x_ref, h_ref, i1_ref, i2_ref, g_ref, gf_ref, u_ref, v_ref, o_ref, w_scr, acc_scr, *,
                   final_norm):
    c = pl.program_id(1)
    tb = x_ref.shape[0]
    n_sub = PEER_EXPERT_CHUNK // PEER_N_KEYS

    @pl.when(c == 0)
    def _():
        sub = lax.broadcasted_iota(jnp.int32, (PEER_N_KEYS, PEER_N_KEYS), 0)

        def build(t, carry):
            i1_row = i1_ref[pl.ds(t, 1), :]
            i2_row = i2_ref[pl.ds(t, 1), :]
            g_row = g_ref[pl.ds(t, 1), :]
            p_t = jnp.where(sub == i1_row, g_row, 0.0).astype(BF16)
            q_t = jnp.where(sub == i2_row, 1.0, 0.0).astype(BF16)
            w_t = lax.dot_general(p_t, q_t, NT, preferred_element_type=F32)
            w_scr[pl.ds(pl.multiple_of(t * W_PITCH, SUBLANES), PEER_N_KEYS), :] = w_t
            return carry

        lax.fori_loop(0, tb, build, 0)
        acc_scr[...] = jnp.zeros(acc_scr.shape, F32)

    act = lax.dot_general(h_ref[...], u_ref[...], NT, preferred_element_type=F32)
    gel = 0.5 * act * (1.0 + lax.erf(act * math.sqrt(0.5)))
    parts = []
    for a in range(n_sub):
        w_a = w_scr[pl.ds(c * n_sub + a, tb, stride=W_PITCH), :]
        parts.append((w_a * gel[:, a * PEER_N_KEYS:(a + 1) * PEER_N_KEYS]).astype(BF16))
    wa = jnp.concatenate(parts, axis=1)
    acc_scr[...] += jnp.dot(wa, v_ref[...], preferred_element_type=F32)

    @pl.when(c == pl.num_programs(1) - 1)
    def _():
        y = x_ref[...] + acc_scr[...]
        if final_norm:
            ms = jnp.mean(y * y, axis=-1, keepdims=True)
            y = y * lax.rsqrt(ms + EPS) * gf_ref[...]
        o_ref[...] = y


def _experts(x, h, i1, i2, g, g_final, u, v, *, tb, final_norm):
    t = x.shape[0]
    slots = i1.shape[1]
    row = lambda w: pl.BlockSpec((tb, w), lambda i, c: (i, 0))
    chunk = pl.BlockSpec((PEER_EXPERT_CHUNK, D_MODEL), lambda i, c: (c, 0))
    return pl.pallas_call(
        functools.partial(_expert_kernel, final_norm=final_norm),
        grid=(t // tb, PEER_EXPERTS // PEER_EXPERT_CHUNK),
        in_specs=[row(D_MODEL), row(D_MODEL), row(slots), row(slots), row(slots),
                  pl.BlockSpec((1, D_MODEL), lambda i, c: (0, 0)), chunk, chunk],
        out_specs=row(D_MODEL),
        out_shape=jax.ShapeDtypeStruct((t, D_MODEL), F32),
        scratch_shapes=[pltpu.VMEM((tb * W_PITCH, PEER_N_KEYS), F32), pltpu.VMEM((tb, D_MODEL), F32)],
        compiler_params=_cparams(("parallel", "arbitrary"), 56),
        name="peer_experts",
    )(x, h, i1, i2, g, g_final.reshape(1, D_MODEL), u, v)


def _split_bf16(w):
    hi = w.astype(BF16)
    return hi, (w - hi.astype(F32)).astype(BF16)


def _layer_tail(x, conv_o, moba_o, diff_o, lw, g_final, *, tm, tb, final_norm):
    x1 = _out_proj(x, conv_o, moba_o, diff_o, lw["wo_c"], lw["wo_m"], lw["wo_d"], tm=min(tm, 256))
    q, h2 = _project(x1, lw["g_ffn"], lw["wq_hi"], lw["wq_lo"], norm=True, emit_h=True, tm=tm, tn=1024)
    i1, i2, g = _route(q, lw["keys"])
    return _experts(x1, h2, i1, i2, g, g_final, lw["u"], lw["v"], tb=tb, final_norm=final_norm)


def kernel(x_prompt, x_sample, cache_moba_k, cache_moba_v, cache_diff_k, cache_diff_v, state_conv, page_table,
           g_mix, w_in, conv_w, diff_lambda_q1, diff_lambda_k1, diff_lambda_q2, diff_lambda_k2, diff_subln_g,
           w_out, g_ffn, peer_w_query, peer_sub_keys, peer_expert_u, peer_expert_v, g_final):
    xp = x_prompt.reshape(N_PROMPT, D_MODEL)
    xs = x_sample.reshape(N_SAMPLE, D_MODEL)
    prompt_state, sample_state = [], []
    for l in range(DEPTH):
        lam_init = 0.8 - 0.6 * math.exp(-0.3 * l)
        final = l == DEPTH - 1
        wq_hi, wq_lo = _split_bf16(peer_w_query[l])
        wo = w_out[l].astype(BF16)
        lw = dict(wo_c=wo[:CONV_CH], wo_m=wo[CONV_CH:CONV_CH + MOBA_HEADS * HEAD_DIM],
                  wo_d=wo[CONV_CH + MOBA_HEADS * HEAD_DIM:], g_ffn=g_ffn[l], wq_hi=wq_hi, wq_lo=wq_lo,
                  keys=peer_sub_keys[l], u=peer_expert_u[l].astype(BF16), v=peer_expert_v[l].astype(BF16))
        w_in_b = w_in[l].astype(BF16)
        lams = [a[l].reshape(1, DIFF_QK_DIM) for a in
                (diff_lambda_q1, diff_lambda_k1, diff_lambda_q2, diff_lambda_k2)]

        proj = _project(xp, g_mix[l], w_in_b, norm=True, tm=512, tn=1024)
        conv_o, conv_new = _conv_prompt(proj, conv_w[l])
        moba_o = _moba_prompt(proj)
        diff_o = _diff_prompt(proj, lams, diff_subln_g[l], lam_init)
        xp = _layer_tail(xp, conv_o, moba_o, diff_o, lw, g_final, tm=512, tb=256, final_norm=final)
        prompt_state.append((proj, conv_new))

        proj_s = _project(xs, g_mix[l], w_in_b, norm=True, tm=N_SAMPLE, tn=1024)
        proj3 = proj_s.reshape(DEC_BATCH, DEC_SEQ, proj_s.shape[1])
        conv_o, conv_new = _conv_sample(proj_s, conv_w[l], state_conv[l])
        moba_o = _moba_sample(proj3, cache_moba_k, cache_moba_v, page_table, l)
        diff_o = _diff_sample(proj3, cache_diff_k, cache_diff_v, page_table, l, lams, diff_subln_g[l], lam_init)
        xs = _layer_tail(xs, conv_o, moba_o.reshape(N_SAMPLE, -1), diff_o.reshape(N_SAMPLE, -1), lw, g_final,
                         tm=N_SAMPLE, tb=N_SAMPLE, final_norm=final)
        sample_state.append((proj_s, conv_new))

    def states(group, b, t):
        cols = lambda c0: jnp.stack([p[:, c0 * LANES:(c0 + MOBA_HEADS) * LANES].reshape(b, t, MOBA_HEADS, HEAD_DIM)
                                     for p, _ in group])
        return cols(COL_MK), cols(COL_MV), cols(COL_DK), cols(COL_DV), jnp.stack([c for _, c in group])

    y_prompt = xp.reshape(BATCH, SEQ, D_MODEL)
    y_sample = xs.reshape(DEC_BATCH, DEC_SEQ, D_MODEL)
    return (y_prompt, y_sample) + states(prompt_state, BATCH, SEQ) + states(sample_state, DEC_BATCH, DEC_SEQ)
```

```python
import functools
import math

import jax
import jax.numpy as jnp
from jax import lax
from jax.experimental import pallas as pl
from jax.experimental.pallas import tpu as pltpu

D_MODEL = 2048
BATCH = 4
SEQ = 2048
DEPTH = 2
DEC_BATCH = 32
DEC_SEQ = 4
PAST_LEN = 8192
PAGE_SIZE = 128
HEAD_DIM = 128
CONV_CH = 512
CONV_WIDTH = 3
MOBA_HEADS = 6
MOBA_BLOCK = 256
MOBA_TOPK = 3
DIFF_HEADS = 6
DIFF_QK_DIM = 64
PEER_HEADS = 8
PEER_N_KEYS = 128
PEER_EXPERTS = PEER_N_KEYS * PEER_N_KEYS
PEER_TOPK = 16
EPS = 1e-6
NEG = -1e30

N_PAGES = PAST_LEN // PAGE_SIZE
N_SAMPLE = DEC_BATCH * DEC_SEQ
N_PROMPT = BATCH * SEQ
N_PROMPT_BLOCKS = SEQ // MOBA_BLOCK
N_PAST_BLOCKS = PAST_LEN // MOBA_BLOCK

COL_MQ, COL_MK, COL_MV = 12, 18, 24
COL_DQ, COL_DK, COL_DV = 30, 36, 42

LANES = 128
SUBLANES = 8
MIB = 1024 * 1024

PAGES_PER_STEP = 16
N_PAGE_STEPS = N_PAGES // PAGES_PER_STEP
KEYS_PER_STEP = PAGES_PER_STEP * PAGE_SIZE
BLOCKS_PER_STEP = KEYS_PER_STEP // MOBA_BLOCK

PEER_EXPERT_CHUNK = 1024
W_PAD = SUBLANES

F32 = jnp.float32
BF16 = jnp.bfloat16
HI = lax.Precision.HIGHEST
NT = (((1,), (1,)), ((), ()))


def _cparams(sem, vmem_mib):
    return pltpu.CompilerParams(dimension_semantics=sem, vmem_limit_bytes=vmem_mib * MIB)


def _proj_kernel(x_ref, g_ref, *rest, norm, split, emit_h):
    if split:
        w_hi_ref, w_lo_ref = rest[0], rest[1]
        rest = rest[2:]
    else:
        w_hi_ref = rest[0]
        rest = rest[1:]
    o_ref = rest[0]
    rest = rest[1:]
    if emit_h:
        hout_ref = rest[0]
        rest = rest[1:]
    h_hi = rest[0]
    h_lo = rest[1] if split else None

    @pl.when(pl.program_id(1) == 0)
    def _():
        x = x_ref[...]
        if norm:
            ms = jnp.mean(x * x, axis=-1, keepdims=True)
            x = x * lax.rsqrt(ms + EPS) * g_ref[...]
        hb = x.astype(BF16)
        h_hi[...] = hb
        if split:
            h_lo[...] = (x - hb.astype(F32)).astype(BF16)
        if emit_h:
            hout_ref[...] = hb

    acc = jnp.dot(h_hi[...], w_hi_ref[...], preferred_element_type=F32)
    if split:
        acc += jnp.dot(h_hi[...], w_lo_ref[...], preferred_element_type=F32)
        acc += jnp.dot(h_lo[...], w_hi_ref[...], preferred_element_type=F32)
    o_ref[...] = acc


def _project(x, g, w_hi, w_lo=None, *, norm=True, emit_h=False, tm, tn):
    t, d = x.shape
    n = w_hi.shape[1]
    split = w_lo is not None
    in_specs = [pl.BlockSpec((tm, d), lambda i, j: (i, 0)),
                pl.BlockSpec((1, d), lambda i, j: (0, 0)),
                pl.BlockSpec((d, tn), lambda i, j: (0, j))]
    args = [x, g.reshape(1, d), w_hi]
    if split:
        in_specs.append(pl.BlockSpec((d, tn), lambda i, j: (0, j)))
        args.append(w_lo)
    out_shape = [jax.ShapeDtypeStruct((t, n), F32)]
    out_specs = [pl.BlockSpec((tm, tn), lambda i, j: (i, j))]
    if emit_h:
        out_shape.append(jax.ShapeDtypeStruct((t, d), BF16))
        out_specs.append(pl.BlockSpec((tm, d), lambda i, j: (i, 0)))
    scratch = [pltpu.VMEM((tm, d), BF16)]
    if split:
        scratch.append(pltpu.VMEM((tm, d), BF16))
    res = pl.pallas_call(
        functools.partial(_proj_kernel, norm=norm, split=split, emit_h=emit_h),
        grid=(t // tm, n // tn),
        in_specs=in_specs, out_specs=out_specs, out_shape=out_shape,
        scratch_shapes=scratch,
        compiler_params=_cparams(("parallel", "arbitrary"), 48),
        name="proj_split" if split else "proj",
    )(*args)
    return res if emit_h else res[0]


def _conv_prompt_kernel(cb_ref, cc_ref, cx_ref, w_ref, y_ref, st_ref, u_scr):
    t = cb_ref.shape[0]
    u = cc_ref[...] * cx_ref[...]
    u_scr[0:SUBLANES, :] = jnp.zeros((SUBLANES, CONV_CH), F32)
    u_scr[SUBLANES:SUBLANES + t, :] = u
    w = w_ref[...]
    y = w[0:1, :] * u_scr[SUBLANES - 2:SUBLANES - 2 + t, :]
    y = y + w[1:2, :] * u_scr[SUBLANES - 1:SUBLANES - 1 + t, :]
    y = y + w[2:3, :] * u
    y_ref[...] = cb_ref[...] * y
    st_ref[...] = u_scr[SUBLANES + t - 2:SUBLANES + t, :]


def _conv_prompt(proj, conv_w):
    return pl.pallas_call(
        _conv_prompt_kernel,
        grid=(BATCH,),
        in_specs=[pl.BlockSpec((SEQ, CONV_CH), lambda b: (b, 0)),
                  pl.BlockSpec((SEQ, CONV_CH), lambda b: (b, 1)),
                  pl.BlockSpec((SEQ, CONV_CH), lambda b: (b, 2)),
                  pl.BlockSpec((CONV_WIDTH, CONV_CH), lambda b: (0, 0))],
        out_specs=[pl.BlockSpec((SEQ, CONV_CH), lambda b: (b, 0)),
                   pl.BlockSpec((None, CONV_WIDTH - 1, CONV_CH), lambda b: (b, 0, 0))],
        out_shape=[jax.ShapeDtypeStruct((N_PROMPT, CONV_CH), F32),
                   jax.ShapeDtypeStruct((BATCH, CONV_WIDTH - 1, CONV_CH), F32)],
        scratch_shapes=[pltpu.VMEM((SEQ + SUBLANES, CONV_CH), F32)],
        compiler_params=_cparams(("parallel",), 48),
        name="conv_prompt",
    )(proj, proj, proj, conv_w)


def _conv_sample_kernel(cb_ref, cc_ref, cx_ref, w_ref, st_in_ref, y_ref, st_ref, u_scr):
    w = w_ref[...]
    for s in range(DEC_BATCH):
        r = s * DEC_SEQ
        u_scr[0:2, :] = st_in_ref[s]
        u_scr[2:2 + DEC_SEQ, :] = cc_ref[r:r + DEC_SEQ, :] * cx_ref[r:r + DEC_SEQ, :]
        y = w[0:1, :] * u_scr[0:DEC_SEQ, :]
        y = y + w[1:2, :] * u_scr[1:1 + DEC_SEQ, :]
        y = y + w[2:3, :] * u_scr[2:2 + DEC_SEQ, :]
        y_ref[r:r + DEC_SEQ, :] = cb_ref[r:r + DEC_SEQ, :] * y
        st_ref[s] = u_scr[DEC_SEQ:DEC_SEQ + 2, :]


def _conv_sample(proj, conv_w, state):
    return pl.pallas_call(
        _conv_sample_kernel,
        grid=(1,),
        in_specs=[pl.BlockSpec((N_SAMPLE, CONV_CH), lambda i: (0, 0)),
                  pl.BlockSpec((N_SAMPLE, CONV_CH), lambda i: (0, 1)),
                  pl.BlockSpec((N_SAMPLE, CONV_CH), lambda i: (0, 2)),
                  pl.BlockSpec((CONV_WIDTH, CONV_CH), lambda i: (0, 0)),
                  pl.BlockSpec((DEC_BATCH, CONV_WIDTH - 1, CONV_CH), lambda i: (0, 0, 0))],
        out_specs=[pl.BlockSpec((N_SAMPLE, CONV_CH), lambda i: (0, 0)),
                   pl.BlockSpec((DEC_BATCH, CONV_WIDTH - 1, CONV_CH), lambda i: (0, 0, 0))],
        out_shape=[jax.ShapeDtypeStruct((N_SAMPLE, CONV_CH), F32),
                   jax.ShapeDtypeStruct((DEC_BATCH, CONV_WIDTH - 1, CONV_CH), F32)],
        scratch_shapes=[pltpu.VMEM((SUBLANES, CONV_CH), F32)],
        name="conv_sample",
    )(proj, proj, proj, conv_w, state)


def _rank_lt(gate, n_cols, k):
    col_id = lax.broadcasted_iota(jnp.int32, gate.shape, 1)
    rank = jnp.zeros(gate.shape, jnp.int32)
    for n2 in range(n_cols):
        other = gate[:, n2:n2 + 1]
        beats = jnp.where(other > gate, 1, jnp.where(other == gate, jnp.where(col_id > n2, 1, 0), 0))
        rank = rank + beats
    return jnp.where(rank < k, 1, 0)


def _moba_prompt_kernel(q_ref, k_ref, v_ref, o_ref, kb_scr, vb_scr, km_scr, m_scr, l_scr, acc_scr):
    i = pl.program_id(2)
    tq = MOBA_BLOCK
    scale = HEAD_DIM ** -0.5

    @pl.when(i == 0)
    def _():
        k = k_ref[...]
        kb_scr[...] = k.astype(BF16)
        vb_scr[...] = v_ref[...].astype(BF16)
        km_scr[...] = jnp.sum(k.reshape(N_PROMPT_BLOCKS, MOBA_BLOCK, HEAD_DIM), axis=1) * (1.0 / MOBA_BLOCK)

    q = q_ref[...]
    gate = lax.dot_general(q, km_scr[...], NT, precision=HI, preferred_element_type=F32)
    blk = lax.broadcasted_iota(jnp.int32, gate.shape, 1)
    gate = jnp.where(blk < i, gate, NEG)
    sel = _rank_lt(gate, N_PROMPT_BLOCKS, MOBA_TOPK) * jnp.where(blk < i, 1, 0)
    qb = q.astype(BF16)

    own = pl.multiple_of(i * tq, tq)
    s = lax.dot_general(qb, kb_scr[pl.ds(own, tq), :], NT, preferred_element_type=F32) * scale
    row = lax.broadcasted_iota(jnp.int32, (tq, tq), 0)
    col = lax.broadcasted_iota(jnp.int32, (tq, tq), 1)
    s = jnp.where(col <= row, s, NEG)
    m = jnp.max(s, axis=-1, keepdims=True)
    p = jnp.exp(s - m)
    m_scr[...] = m
    l_scr[...] = jnp.sum(p, axis=-1, keepdims=True)
    acc_scr[...] = jnp.dot(p.astype(BF16), vb_scr[pl.ds(own, tq), :], preferred_element_type=F32)

    for n in range(N_PROMPT_BLOCKS - 1):
        @pl.when(n < i)
        def _(n=n):
            sn = lax.dot_general(qb, kb_scr[n * tq:(n + 1) * tq, :], NT, preferred_element_type=F32) * scale
            sn = jnp.where(sel[:, n:n + 1] > 0, sn, NEG)
            m_old = m_scr[...]
            m_new = jnp.maximum(m_old, jnp.max(sn, axis=-1, keepdims=True))
            alpha = jnp.exp(m_old - m_new)
            pn = jnp.exp(sn - m_new)
            l_scr[...] = alpha * l_scr[...] + jnp.sum(pn, axis=-1, keepdims=True)
            acc_scr[...] = alpha * acc_scr[...] + jnp.dot(pn.astype(BF16), vb_scr[n * tq:(n + 1) * tq, :],
                                                          preferred_element_type=F32)
            m_scr[...] = m_new

    o_ref[...] = acc_scr[...] / l_scr[...]


def _moba_prompt(proj):
    tq = MOBA_BLOCK
    return pl.pallas_call(
        _moba_prompt_kernel,
        grid=(BATCH, MOBA_HEADS, N_PROMPT_BLOCKS),
        in_specs=[pl.BlockSpec((tq, HEAD_DIM), lambda b, h, i: (b * N_PROMPT_BLOCKS + i, COL_MQ + h)),
                  pl.BlockSpec((SEQ, HEAD_DIM), lambda b, h, i: (b, COL_MK + h)),
                  pl.BlockSpec((SEQ, HEAD_DIM), lambda b, h, i: (b, COL_MV + h))],
        out_specs=pl.BlockSpec((tq, HEAD_DIM), lambda b, h, i: (b * N_PROMPT_BLOCKS + i, h)),
        out_shape=jax.ShapeDtypeStruct((N_PROMPT, MOBA_HEADS * HEAD_DIM), F32),
        scratch_shapes=[pltpu.VMEM((SEQ, HEAD_DIM), BF16), pltpu.VMEM((SEQ, HEAD_DIM), BF16),
                        pltpu.VMEM((N_PROMPT_BLOCKS, HEAD_DIM), F32),
                        pltpu.VMEM((tq, 1), F32), pltpu.VMEM((tq, 1), F32), pltpu.VMEM((tq, HEAD_DIM), F32)],
        compiler_params=_cparams(("parallel", "parallel", "arbitrary"), 32),
        name="moba_prompt",
    )(proj, proj, proj)


def _diff_lambda(lq1_ref, lk1_ref, lq2_ref, lk2_ref, lam_init):
    a = jnp.sum(lq1_ref[...] * lk1_ref[...], axis=-1, keepdims=True)
    b = jnp.sum(lq2_ref[...] * lk2_ref[...], axis=-1, keepdims=True)
    return jnp.exp(a) - jnp.exp(b) + lam_init


def _diff_finish(o1, o2, lam, g, lam_init):
    o = o1 - lam * o2
    ms = jnp.mean(o * o, axis=-1, keepdims=True)
    return o * lax.rsqrt(ms + EPS) * g * (1.0 - lam_init)


def _split_maps(q):
    lane = lax.broadcasted_iota(jnp.int32, q.shape, 1)
    return jnp.where(lane < DIFF_QK_DIM, q, 0.0), jnp.where(lane >= DIFF_QK_DIM, q, 0.0)


def _diff_prompt_kernel(lq1_ref, lk1_ref, lq2_ref, lk2_ref, g_ref, q_ref, k_ref, v_ref, o_ref,
                        kb_scr, vb_scr, m_scr, l_scr, acc_scr, *, lam_init):
    i = pl.program_id(2)
    tq = MOBA_BLOCK
    scale = DIFF_QK_DIM ** -0.5

    @pl.when(i == 0)
    def _():
        kb_scr[...] = k_ref[...].astype(BF16)
        vb_scr[...] = v_ref[...].astype(BF16)

    q1, q2 = _split_maps(q_ref[...])
    qs = (q1.astype(BF16), q2.astype(BF16))
    own = pl.multiple_of(i * tq, tq)
    row = lax.broadcasted_iota(jnp.int32, (tq, tq), 0)
    col = lax.broadcasted_iota(jnp.int32, (tq, tq), 1)
    for c in range(2):
        s = lax.dot_general(qs[c], kb_scr[pl.ds(own, tq), :], NT, preferred_element_type=F32) * scale
        s = jnp.where(col <= row, s, NEG)
        m = jnp.max(s, axis=-1, keepdims=True)
        p = jnp.exp(s - m)
        m_scr[c] = m
        l_scr[c] = jnp.sum(p, axis=-1, keepdims=True)
        acc_scr[c] = jnp.dot(p.astype(BF16), vb_scr[pl.ds(own, tq), :], preferred_element_type=F32)

    for n in range(N_PROMPT_BLOCKS - 1):
        @pl.when(n < i)
        def _(n=n):
            kb = kb_scr[n * tq:(n + 1) * tq, :]
            vb = vb_scr[n * tq:(n + 1) * tq, :]
            for c in range(2):
                sn = lax.dot_general(qs[c], kb, NT, preferred_element_type=F32) * scale
                m_old = m_scr[c]
                m_new = jnp.maximum(m_old, jnp.max(sn, axis=-1, keepdims=True))
                alpha = jnp.exp(m_old - m_new)
                pn = jnp.exp(sn - m_new)
                l_scr[c] = alpha * l_scr[c] + jnp.sum(pn, axis=-1, keepdims=True)
                acc_scr[c] = alpha * acc_scr[c] + jnp.dot(pn.astype(BF16), vb, preferred_element_type=F32)
                m_scr[c] = m_new

    lam = _diff_lambda(lq1_ref, lk1_ref, lq2_ref, lk2_ref, lam_init)
    o_ref[...] = _diff_finish(acc_scr[0] / l_scr[0], acc_scr[1] / l_scr[1], lam, g_ref[...], lam_init)


def _diff_prompt(proj, lams, subln_g, lam_init):
    tq = MOBA_BLOCK
    lam_specs = [pl.BlockSpec((1, DIFF_QK_DIM), lambda b, h, i: (0, 0))] * 4
    return pl.pallas_call(
        functools.partial(_diff_prompt_kernel, lam_init=lam_init),
        grid=(BATCH, DIFF_HEADS, N_PROMPT_BLOCKS),
        in_specs=lam_specs + [
            pl.BlockSpec((1, HEAD_DIM), lambda b, h, i: (0, 0)),
            pl.BlockSpec((tq, HEAD_DIM), lambda b, h, i: (b * N_PROMPT_BLOCKS + i, COL_DQ + h)),
            pl.BlockSpec((SEQ, HEAD_DIM), lambda b, h, i: (b, COL_DK + h)),
            pl.BlockSpec((SEQ, HEAD_DIM), lambda b, h, i: (b, COL_DV + h))],
        out_specs=pl.BlockSpec((tq, HEAD_DIM), lambda b, h, i: (b * N_PROMPT_BLOCKS + i, h)),
        out_shape=jax.ShapeDtypeStruct((N_PROMPT, DIFF_HEADS * HEAD_DIM), F32),
        scratch_shapes=[pltpu.VMEM((SEQ, HEAD_DIM), BF16), pltpu.VMEM((SEQ, HEAD_DIM), BF16),
                        pltpu.VMEM((2, tq, 1), F32), pltpu.VMEM((2, tq, 1), F32),
                        pltpu.VMEM((2, tq, HEAD_DIM), F32)],
        compiler_params=_cparams(("parallel", "parallel", "arbitrary"), 32),
        name="diff_prompt",
    )(*lams, subln_g.reshape(1, HEAD_DIM), proj, proj, proj)


def _page_specs(layer, pinned_phase=None):
    specs = []
    for r in range(PAGES_PER_STEP):
        if pinned_phase is None:
            def imap(b, s, pt, r=r):
                return (layer, pt[b, s * PAGES_PER_STEP + r], 0, 0, 0)
        elif pinned_phase == 1:
            def imap(b, ph, s, pt, r=r):
                st = jnp.where(ph == 0, s, N_PAGE_STEPS - 1)
                return (layer, pt[b, st * PAGES_PER_STEP + r], 0, 0, 0)
        else:
            def imap(b, ph, s, pt, r=r):
                st = jnp.where(ph == 0, 0, s)
                return (layer, pt[b, st * PAGES_PER_STEP + r], 0, 0, 0)
        specs.append(pl.BlockSpec((None, None, MOBA_HEADS, PAGE_SIZE, HEAD_DIM), imap))
    return specs


def _head_major(cache):
    return jnp.transpose(cache, (0, 1, 3, 2, 4))


def _load_pages(refs, h):
    return jnp.concatenate([r[h] for r in refs], axis=0)


def _new_key_mask():
    r = lax.broadcasted_iota(jnp.int32, (SUBLANES, PAGE_SIZE), 0)
    j = lax.broadcasted_iota(jnp.int32, (SUBLANES, PAGE_SIZE), 1)
    return (j <= (r & (DEC_SEQ - 1))) & (j < DEC_SEQ)


def _stage_new_kv(kn_ref, vn_ref, knp_scr, vnp_scr, h):
    knp_scr[...] = jnp.zeros(knp_scr.shape, F32)
    vnp_scr[...] = jnp.zeros(vnp_scr.shape, F32)
    knp_scr[0:DEC_SEQ, :] = kn_ref[:, h * HEAD_DIM:(h + 1) * HEAD_DIM]
    vnp_scr[0:DEC_SEQ, :] = vn_ref[:, h * HEAD_DIM:(h + 1) * HEAD_DIM]


def _diff_sample_kernel(pt_ref, lq1_ref, lk1_ref, lq2_ref, lk2_ref, g_ref, q_ref, kn_ref, vn_ref, *rest,
                        lam_init):
    k_refs = rest[:PAGES_PER_STEP]
    v_refs = rest[PAGES_PER_STEP:2 * PAGES_PER_STEP]
    o_ref = rest[2 * PAGES_PER_STEP]
    q8_scr, knp_scr, vnp_scr, m_scr, l_scr, acc_scr = rest[2 * PAGES_PER_STEP + 1:]
    step = pl.program_id(1)
    scale = DIFF_QK_DIM ** -0.5

    @pl.when(step == 0)
    def _():
        for h in range(DIFF_HEADS):
            q1, q2 = _split_maps(q_ref[:, h * HEAD_DIM:(h + 1) * HEAD_DIM])
            q8_scr[h, 0:DEC_SEQ, :] = q1
            q8_scr[h, DEC_SEQ:2 * DEC_SEQ, :] = q2
        m_scr[...] = jnp.full(m_scr.shape, -jnp.inf, F32)
        l_scr[...] = jnp.zeros(l_scr.shape, F32)
        acc_scr[...] = jnp.zeros(acc_scr.shape, F32)

    def update(h, s, vb):
        m_old = m_scr[h]
        m_new = jnp.maximum(m_old, jnp.max(s, axis=-1, keepdims=True))
        alpha = jnp.exp(m_old - m_new)
        p = jnp.exp(s - m_new)
        l_scr[h] = alpha * l_scr[h] + jnp.sum(p, axis=-1, keepdims=True)
        acc_scr[h] = alpha * acc_scr[h] + jnp.dot(p.astype(BF16), vb, preferred_element_type=F32)
        m_scr[h] = m_new

    for h in range(DIFF_HEADS):
        q8 = q8_scr[h].astype(BF16)
        kb = _load_pages(k_refs, h).astype(BF16)
        vb = _load_pages(v_refs, h).astype(BF16)
        update(h, lax.dot_general(q8, kb, NT, preferred_element_type=F32) * scale, vb)

    @pl.when(step == N_PAGE_STEPS - 1)
    def _():
        lam = _diff_lambda(lq1_ref, lk1_ref, lq2_ref, lk2_ref, lam_init)
        for h in range(DIFF_HEADS):
            q8 = q8_scr[h].astype(BF16)
            _stage_new_kv(kn_ref, vn_ref, knp_scr, vnp_scr, h)
            sn = lax.dot_general(q8, knp_scr[...].astype(BF16), NT, preferred_element_type=F32) * scale
            update(h, jnp.where(_new_key_mask(), sn, NEG), vnp_scr[...].astype(BF16))
            o = acc_scr[h] / l_scr[h]
            o_ref[:, h * HEAD_DIM:(h + 1) * HEAD_DIM] = _diff_finish(
                o[0:DEC_SEQ, :], o[DEC_SEQ:2 * DEC_SEQ, :], lam, g_ref[...], lam_init)


def _tok_spec(c0):
    width = MOBA_HEADS * HEAD_DIM
    return pl.BlockSpec((None, DEC_SEQ, width), lambda b, *_: (b, 0, c0 // MOBA_HEADS))


def _diff_sample(proj3, cache_k, cache_v, page_table, layer, lams, subln_g, lam_init):
    lam_specs = [pl.BlockSpec((1, DIFF_QK_DIM), lambda b, s, pt: (0, 0))] * 4
    grid_spec = pltpu.PrefetchScalarGridSpec(
        num_scalar_prefetch=1,
        grid=(DEC_BATCH, N_PAGE_STEPS),
        in_specs=lam_specs + [pl.BlockSpec((1, HEAD_DIM), lambda b, s, pt: (0, 0)),
                              _tok_spec(COL_DQ), _tok_spec(COL_DK), _tok_spec(COL_DV)]
        + _page_specs(layer) + _page_specs(layer),
        out_specs=pl.BlockSpec((None, DEC_SEQ, DIFF_HEADS * HEAD_DIM), lambda b, s, pt: (b, 0, 0)),
        scratch_shapes=[pltpu.VMEM((DIFF_HEADS, SUBLANES, HEAD_DIM), F32),
                        pltpu.VMEM((PAGE_SIZE, HEAD_DIM), F32), pltpu.VMEM((PAGE_SIZE, HEAD_DIM), F32),
                        pltpu.VMEM((DIFF_HEADS, SUBLANES, 1), F32), pltpu.VMEM((DIFF_HEADS, SUBLANES, 1), F32),
                        pltpu.VMEM((DIFF_HEADS, SUBLANES, HEAD_DIM), F32)])
    return pl.pallas_call(
        functools.partial(_diff_sample_kernel, lam_init=lam_init),
        grid_spec=grid_spec,
        out_shape=jax.ShapeDtypeStruct((DEC_BATCH, DEC_SEQ, DIFF_HEADS * HEAD_DIM), F32),
        compiler_params=_cparams(("parallel", "arbitrary"), 56),
        name="diff_sample",
    )(page_table, *lams, subln_g.reshape(1, HEAD_DIM), proj3, proj3, proj3,
      *([cache_k] * PAGES_PER_STEP), *([cache_v] * PAGES_PER_STEP))


def _moba_sample_kernel(pt_ref, q_ref, kn_ref, vn_ref, *rest):
    k_refs = rest[:PAGES_PER_STEP]
    v_refs = rest[PAGES_PER_STEP:2 * PAGES_PER_STEP]
    o_ref = rest[2 * PAGES_PER_STEP]
    q8_scr, knp_scr, vnp_scr, s_scr, km_scr, p_scr, l_scr, acc_scr = rest[2 * PAGES_PER_STEP + 1:]
    phase = pl.program_id(1)
    step = pl.program_id(2)
    scale = HEAD_DIM ** -0.5

    @pl.when((phase == 0) & (step == 0))
    def _():
        q8_scr[...] = jnp.zeros(q8_scr.shape, F32)
        for h in range(MOBA_HEADS):
            q8_scr[h, 0:DEC_SEQ, :] = q_ref[:, h * HEAD_DIM:(h + 1) * HEAD_DIM]

    @pl.when(phase == 0)
    def _():
        for h in range(MOBA_HEADS):
            k = _load_pages(k_refs, h)
            km_scr[h, step] = (jnp.sum(k.reshape(BLOCKS_PER_STEP, MOBA_BLOCK, HEAD_DIM), axis=1)
                               * (1.0 / MOBA_BLOCK))
            s_scr[h, step] = lax.dot_general(q8_scr[h].astype(BF16), k.astype(BF16), NT,
                                             preferred_element_type=F32) * scale

    @pl.when((phase == 1) & (step == 0))
    def _():
        for h in range(MOBA_HEADS):
            q8 = q8_scr[h]
            km = km_scr[h].reshape(N_PAST_BLOCKS, HEAD_DIM)
            gate = lax.dot_general(q8, km, NT, precision=HI, preferred_element_type=F32)
            sel = _rank_lt(gate, N_PAST_BLOCKS, MOBA_TOPK)
            _stage_new_kv(kn_ref, vn_ref, knp_scr, vnp_scr, h)
            sn = lax.dot_general(q8.astype(BF16), knp_scr[...].astype(BF16), NT,
                                 preferred_element_type=F32) * scale
            sn = jnp.where(_new_key_mask(), sn, NEG)
            m = jnp.max(sn, axis=-1, keepdims=True)

            def block(n, h=h, sel=sel):
                st, jb = divmod(n, BLOCKS_PER_STEP)
                return jnp.where(sel[:, n:n + 1] > 0,
                                 s_scr[h, st, :, jb * MOBA_BLOCK:(jb + 1) * MOBA_BLOCK], NEG)

            for n in range(N_PAST_BLOCKS):
                m = jnp.maximum(m, jnp.max(block(n), axis=-1, keepdims=True))
            pn = jnp.exp(sn - m)
            l = jnp.sum(pn, axis=-1, keepdims=True)
            for n in range(N_PAST_BLOCKS):
                st, jb = divmod(n, BLOCKS_PER_STEP)
                pb = jnp.exp(block(n) - m)
                l = l + jnp.sum(pb, axis=-1, keepdims=True)
                p_scr[h, st, :, jb * MOBA_BLOCK:(jb + 1) * MOBA_BLOCK] = pb
            l_scr[h] = l
            acc_scr[h] = jnp.dot(pn.astype(BF16), vnp_scr[...].astype(BF16), preferred_element_type=F32)

    @pl.when(phase == 1)
    def _():
        for h in range(MOBA_HEADS):
            vb = _load_pages(v_refs, h).astype(BF16)
            acc_scr[h] += jnp.dot(p_scr[h, step].astype(BF16), vb, preferred_element_type=F32)

    @pl.when((phase == 1) & (step == N_PAGE_STEPS - 1))
    def _():
        for h in range(MOBA_HEADS):
            o = acc_scr[h] / l_scr[h]
            o_ref[:, h * HEAD_DIM:(h + 1) * HEAD_DIM] = o[0:DEC_SEQ, :]


def _moba_sample(proj3, cache_k, cache_v, page_table, layer):
    grid_spec = pltpu.PrefetchScalarGridSpec(
        num_scalar_prefetch=1,
        grid=(DEC_BATCH, 2, N_PAGE_STEPS),
        in_specs=[_tok_spec(COL_MQ), _tok_spec(COL_MK), _tok_spec(COL_MV)]
        + _page_specs(layer, pinned_phase=1) + _page_specs(layer, pinned_phase=0),
        out_specs=pl.BlockSpec((None, DEC_SEQ, MOBA_HEADS * HEAD_DIM), lambda b, ph, s, pt: (b, 0, 0)),
        scratch_shapes=[pltpu.VMEM((MOBA_HEADS, SUBLANES, HEAD_DIM), F32),
                        pltpu.VMEM((PAGE_SIZE, HEAD_DIM), F32), pltpu.VMEM((PAGE_SIZE, HEAD_DIM), F32),
                        pltpu.VMEM((MOBA_HEADS, N_PAGE_STEPS, SUBLANES, KEYS_PER_STEP), F32),
                        pltpu.VMEM((MOBA_HEADS, N_PAGE_STEPS, BLOCKS_PER_STEP, HEAD_DIM), F32),
                        pltpu.VMEM((MOBA_HEADS, N_PAGE_STEPS, SUBLANES, KEYS_PER_STEP), F32),
                        pltpu.VMEM((MOBA_HEADS, SUBLANES, 1), F32),
                        pltpu.VMEM((MOBA_HEADS, SUBLANES, HEAD_DIM), F32)])
    return pl.pallas_call(
        _moba_sample_kernel,
        grid_spec=grid_spec,
        out_shape=jax.ShapeDtypeStruct((DEC_BATCH, DEC_SEQ, MOBA_HEADS * HEAD_DIM), F32),
        compiler_params=_cparams(("parallel", "arbitrary", "arbitrary"), 56),
        name="moba_sample",
    )(page_table, proj3, proj3, proj3, *([cache_k] * PAGES_PER_STEP), *([cache_v] * PAGES_PER_STEP))


def _out_proj_kernel(x_ref, c_ref, m_ref, d_ref, wc_ref, wm_ref, wd_ref, o_ref):
    acc = jnp.dot(c_ref[...].astype(BF16), wc_ref[...], preferred_element_type=F32)
    acc += jnp.dot(m_ref[...].astype(BF16), wm_ref[...], preferred_element_type=F32)
    acc += jnp.dot(d_ref[...].astype(BF16), wd_ref[...], preferred_element_type=F32)
    o_ref[...] = x_ref[...] + acc


def _out_proj(x, conv_o, moba_o, diff_o, wc, wm, wd, tm):
    t = x.shape[0]
    row = lambda w: pl.BlockSpec((tm, w), lambda i: (i, 0))
    full = lambda a: pl.BlockSpec(a.shape, lambda i: (0, 0))
    return pl.pallas_call(
        _out_proj_kernel,
        grid=(t // tm,),
        in_specs=[row(D_MODEL), row(conv_o.shape[1]), row(moba_o.shape[1]), row(diff_o.shape[1]),
                  full(wc), full(wm), full(wd)],
        out_specs=row(D_MODEL),
        out_shape=jax.ShapeDtypeStruct((t, D_MODEL), F32),
        compiler_params=_cparams(("parallel",), 48),
        name="out_proj",
    )(x, conv_o, moba_o, diff_o, wc, wm, wd)


def _top16_rows(s, row_id, id_bound):
    t = s.shape[1]
    slot = lax.broadcasted_iota(jnp.int32, (PEER_TOPK, t), 0)

    def body(k, carry):
        s, vals, idxs = carry
        m = jnp.max(s, axis=0, keepdims=True)
        idx = jnp.min(jnp.where(s == m, row_id, id_bound), axis=0, keepdims=True)
        vals = jnp.where(slot == k, m, vals)
        idxs = jnp.where(slot == k, idx, idxs)
        s = jnp.where(row_id == idx, -jnp.inf, s)
        return s, vals, idxs

    _, vals, idxs = lax.fori_loop(0, PEER_TOPK, body,
                                  (s, jnp.zeros((PEER_TOPK, t), F32), jnp.zeros((PEER_TOPK, t), jnp.int32)))
    return vals, idxs


def _route_kernel(q_ref, keys_ref, i1_ref, i2_ref, g_ref):
    t = q_ref.shape[0]
    key_id = lax.broadcasted_iota(jnp.int32, (PEER_N_KEYS, t), 0)
    r16 = lax.broadcasted_iota(jnp.int32, (PEER_TOPK, t), 0)
    r8 = lax.broadcasted_iota(jnp.int32, (SUBLANES, t), 0)
    pair_id = jnp.concatenate([r16] + [k1 * PEER_TOPK + r8 for k1 in range(1, SUBLANES)]
                              + [(r8 + SUBLANES) * PEER_TOPK], axis=0)
    i1_parts, i2_parts, g_parts = [], [], []
    for h in range(PEER_HEADS):
        tops = []
        for p in range(2):
            c0 = (h * 2 + p) * PEER_N_KEYS
            q_hp = q_ref[:, c0:c0 + PEER_N_KEYS]
            s_t = lax.dot_general(keys_ref[h, p], q_hp, NT, precision=HI, preferred_element_type=F32)
            tops.append(_top16_rows(s_t, key_id, PEER_N_KEYS))
        (v0, x0), (v1, x1) = tops
        cand = jnp.concatenate([v0[0:1, :] + v1]
                               + [v0[k1:k1 + 1, :] + v1[0:SUBLANES, :] for k1 in range(1, SUBLANES)]
                               + [v0[SUBLANES:PEER_TOPK, :] + v1[0:1, :]], axis=0)
        best, pos = _top16_rows(cand, pair_id, PEER_TOPK * PEER_TOPK)
        k1 = pos >> 4
        k2 = pos & (PEER_TOPK - 1)
        a = jnp.zeros(pos.shape, jnp.int32)
        b = jnp.zeros(pos.shape, jnp.int32)
        for k in range(PEER_TOPK):
            a = jnp.where(k1 == k, x0[k:k + 1, :], a)
            b = jnp.where(k2 == k, x1[k:k + 1, :], b)
        e = jnp.exp(best - jnp.max(best, axis=0, keepdims=True))
        g_parts.append(e / jnp.sum(e, axis=0, keepdims=True))
        i1_parts.append(a)
        i2_parts.append(b)
    i1_ref[...] = jnp.concatenate(i1_parts, axis=0).T
    i2_ref[...] = jnp.concatenate(i2_parts, axis=0).T
    g_ref[...] = jnp.concatenate(g_parts, axis=0).T


def _route(q, keys):
    t = q.shape[0]
    tb = LANES
    slots = PEER_HEADS * PEER_TOPK
    spec = pl.BlockSpec((tb, slots), lambda i: (i, 0))
    return pl.pallas_call(
        _route_kernel,
        grid=(t // tb,),
        in_specs=[pl.BlockSpec((tb, q.shape[1]), lambda i: (i, 0)),
                  pl.BlockSpec(keys.shape, lambda i: (0, 0, 0, 0))],
        out_specs=[spec, spec, spec],
        out_shape=[jax.ShapeDtypeStruct((t, slots), jnp.int32), jax.ShapeDtypeStruct((t, slots), jnp.int32),
                   jax.ShapeDtypeStruct((t, slots), F32)],
        compiler_params=_cparams(("parallel",), 32),
        name="peer_route",
    )(q, keys)


def _expert_kernel(x_ref, h_ref, i1_ref, i2_ref, g_ref, gf_ref, u_ref, v_ref, o_ref, w_scr, acc_scr, *,
                   final_norm):
    c = pl.program_id(1)
    tb = x_ref.shape[0]
    n_sub = PEER_EXPERT_CHUNK // PEER_N_KEYS
    w_stride = tb + W_PAD

    @pl.when(c == 0)
    def _():
        sub = lax.broadcasted_iota(jnp.int32, (PEER_N_KEYS, PEER_N_KEYS), 0)

        def build(t, carry):
            i1_row = i1_ref[pl.ds(t, 1), :]
            i2_row = i2_ref[pl.ds(t, 1), :]
            g_row = g_ref[pl.ds(t, 1), :]
            p_t = jnp.where(sub == i1_row, g_row, 0.0).astype(BF16)
            q_t = jnp.where(sub == i2_row, 1.0, 0.0).astype(BF16)
            w_t = lax.dot_general(p_t, q_t, NT, preferred_element_type=F32)
            w_scr[pl.ds(t, PEER_N_KEYS, stride=w_stride), :] = w_t
            return carry

        lax.fori_loop(0, tb, build, 0, unroll=8)
        acc_scr[...] = jnp.zeros(acc_scr.shape, F32)

    act = lax.dot_general(h_ref[...], u_ref[...], NT, preferred_element_type=F32)
    gel = 0.5 * act * (1.0 + lax.erf(act * math.sqrt(0.5)))
    parts = []
    for a in range(n_sub):
        w_a = w_scr[pl.ds(pl.multiple_of((c * n_sub + a) * w_stride, SUBLANES), tb), :]
        parts.append((w_a * gel[:, a * PEER_N_KEYS:(a + 1) * PEER_N_KEYS]).astype(BF16))
    wa = jnp.concatenate(parts, axis=1)
    acc_scr[...] += jnp.dot(wa, v_ref[...], preferred_element_type=F32)

    @pl.when(c == pl.num_programs(1) - 1)
    def _():
        y = x_ref[...] + acc_scr[...]
        if final_norm:
            ms = jnp.mean(y * y, axis=-1, keepdims=True)
            y = y * lax.rsqrt(ms + EPS) * gf_ref[...]
        o_ref[...] = y


def _experts(x, h, i1, i2, g, g_final, u, v, *, tb, final_norm):
    t = x.shape[0]
    slots = i1.shape[1]
    row = lambda w: pl.BlockSpec((tb, w), lambda i, c: (i, 0))
    chunk = pl.BlockSpec((PEER_EXPERT_CHUNK, D_MODEL), lambda i, c: (c, 0))
    return pl.pallas_call(
        functools.partial(_expert_kernel, final_norm=final_norm),
        grid=(t // tb, PEER_EXPERTS // PEER_EXPERT_CHUNK),
        in_specs=[row(D_MODEL), row(D_MODEL), row(slots), row(slots), row(slots),
                  pl.BlockSpec((1, D_MODEL), lambda i, c: (0, 0)), chunk, chunk],
        out_specs=row(D_MODEL),
        out_shape=jax.ShapeDtypeStruct((t, D_MODEL), F32),
        scratch_shapes=[pltpu.VMEM((PEER_N_KEYS * (tb + W_PAD), PEER_N_KEYS), F32),
                        pltpu.VMEM((tb, D_MODEL), F32)],
        compiler_params=_cparams(("parallel", "arbitrary"), 56),
        name="peer_experts",
    )(x, h, i1, i2, g, g_final.reshape(1, D_MODEL), u, v)


def _split_bf16(w):
    hi = w.astype(BF16)
    return hi, (w - hi.astype(F32)).astype(BF16)


def _layer_tail(x, conv_o, moba_o, diff_o, lw, g_final, *, tm, tb, final_norm):
    x1 = _out_proj(x, conv_o, moba_o, diff_o, lw["wo_c"], lw["wo_m"], lw["wo_d"], tm=min(tm, 256))
    q, h2 = _project(x1, lw["g_ffn"], lw["wq_hi"], lw["wq_lo"], norm=True, emit_h=True, tm=tm, tn=1024)
    i1, i2, g = _route(q, lw["keys"])
    return _experts(x1, h2, i1, i2, g, g_final, lw["u"], lw["v"], tb=tb, final_norm=final_norm)


def kernel(x_prompt, x_sample, cache_moba_k, cache_moba_v, cache_diff_k, cache_diff_v, state_conv, page_table,
           g_mix, w_in, conv_w, diff_lambda_q1, diff_lambda_k1, diff_lambda_q2, diff_lambda_k2, diff_subln_g,
           w_out, g_ffn, peer_w_query, peer_sub_keys, peer_expert_u, peer_expert_v, g_final):
    xp = x_prompt.reshape(N_PROMPT, D_MODEL)
    xs = x_sample.reshape(N_SAMPLE, D_MODEL)
    cache_moba_k, cache_moba_v, cache_diff_k, cache_diff_v = (
        _head_major(c) for c in (cache_moba_k, cache_moba_v, cache_diff_k, cache_diff_v))
    prompt_state, sample_state = [], []
    for l in range(DEPTH):
        lam_init = 0.8 - 0.6 * math.exp(-0.3 * l)
        final = l == DEPTH - 1
        wq_hi, wq_lo = _split_bf16(peer_w_query[l])
        wo = w_out[l].astype(BF16)
        lw = dict(wo_c=wo[:CONV_CH], wo_m=wo[CONV_CH:CONV_CH + MOBA_HEADS * HEAD_DIM],
                  wo_d=wo[CONV_CH + MOBA_HEADS * HEAD_DIM:], g_ffn=g_ffn[l], wq_hi=wq_hi, wq_lo=wq_lo,
                  keys=peer_sub_keys[l], u=peer_expert_u[l].astype(BF16), v=peer_expert_v[l].astype(BF16))
        w_in_b = w_in[l].astype(BF16)
        lams = [a[l].reshape(1, DIFF_QK_DIM) for a in
                (diff_lambda_q1, diff_lambda_k1, diff_lambda_q2, diff_lambda_k2)]

        proj = _project(xp, g_mix[l], w_in_b, norm=True, tm=512, tn=1024)
        conv_o, conv_new = _conv_prompt(proj, conv_w[l])
        moba_o = _moba_prompt(proj)
        diff_o = _diff_prompt(proj, lams, diff_subln_g[l], lam_init)
        xp = _layer_tail(xp, conv_o, moba_o, diff_o, lw, g_final, tm=512, tb=256, final_norm=final)
        prompt_state.append((proj, conv_new))

        proj_s = _project(xs, g_mix[l], w_in_b, norm=True, tm=N_SAMPLE, tn=1024)
        proj3 = proj_s.reshape(DEC_BATCH, DEC_SEQ, proj_s.shape[1])
        conv_o, conv_new = _conv_sample(proj_s, conv_w[l], state_conv[l])
        moba_o = _moba_sample(proj3, cache_moba_k, cache_moba_v, page_table, l)
        diff_o = _diff_sample(proj3, cache_diff_k, cache_diff_v, page_table, l, lams, diff_subln_g[l], lam_init)
        xs = _layer_tail(xs, conv_o, moba_o.reshape(N_SAMPLE, -1), diff_o.reshape(N_SAMPLE, -1), lw, g_final,
                         tm=N_SAMPLE, tb=N_SAMPLE, final_norm=final)
        sample_state.append((proj_s, conv_new))

    def states(group, b, t):
        cols = lambda c0: jnp.stack([p[:, c0 * LANES:(c0 + MOBA_HEADS) * LANES].reshape(b, t, MOBA_HEADS, HEAD_DIM)
                                     for p, _ in group])
        return cols(COL_MK), cols(COL_MV), cols(COL_DK), cols(COL_DV), jnp.stack([c for _, c in group])

    y_prompt = xp.reshape(BATCH, SEQ, D_MODEL)
    y_sample = xs.reshape(DEC_BATCH, DEC_SEQ, D_MODEL)
    return (y_prompt, y_sample) + states(prompt_state, BATCH, SEQ) + states(sample_state, DEC_BATCH, DEC_SEQ)
```

```python
import functools
import math

import jax
import jax.numpy as jnp
from jax import lax
from jax.experimental import pallas as pl
from jax.experimental.pallas import tpu as pltpu

D_MODEL = 2048
BATCH = 4
SEQ = 2048
DEPTH = 2
DEC_BATCH = 32
DEC_SEQ = 4
PAST_LEN = 8192
PAGE_SIZE = 128
HEAD_DIM = 128
CONV_CH = 512
CONV_WIDTH = 3
MOBA_HEADS = 6
MOBA_BLOCK = 256
MOBA_TOPK = 3
DIFF_HEADS = 6
DIFF_QK_DIM = 64
PEER_HEADS = 8
PEER_N_KEYS = 128
PEER_EXPERTS = PEER_N_KEYS * PEER_N_KEYS
PEER_TOPK = 16
EPS = 1e-6
NEG = -1e30

N_PAGES = PAST_LEN // PAGE_SIZE
N_SAMPLE = DEC_BATCH * DEC_SEQ
N_PROMPT = BATCH * SEQ
N_PROMPT_BLOCKS = SEQ // MOBA_BLOCK
N_PAST_BLOCKS = PAST_LEN // MOBA_BLOCK

COL_MQ, COL_MK, COL_MV = 12, 18, 24
COL_DQ, COL_DK, COL_DV = 30, 36, 42

LANES = 128
SUBLANES = 8
MIB = 1024 * 1024

PAGES_PER_STEP = 16
N_PAGE_STEPS = N_PAGES // PAGES_PER_STEP
KEYS_PER_STEP = PAGES_PER_STEP * PAGE_SIZE
BLOCKS_PER_STEP = KEYS_PER_STEP // MOBA_BLOCK

PEER_EXPERT_CHUNK = 1024
W_PAD = SUBLANES

F32 = jnp.float32
BF16 = jnp.bfloat16
HI = lax.Precision.HIGHEST
NT = (((1,), (1,)), ((), ()))


def _cparams(sem, vmem_mib):
    return pltpu.CompilerParams(dimension_semantics=sem, vmem_limit_bytes=vmem_mib * MIB)


def _proj_kernel(x_ref, g_ref, *rest, norm, split, emit_h):
    if split:
        w_hi_ref, w_lo_ref = rest[0], rest[1]
        rest = rest[2:]
    else:
        w_hi_ref = rest[0]
        rest = rest[1:]
    o_ref = rest[0]
    rest = rest[1:]
    if emit_h:
        hout_ref = rest[0]
        rest = rest[1:]
    h_hi = rest[0]
    h_lo = rest[1] if split else None

    @pl.when(pl.program_id(1) == 0)
    def _():
        x = x_ref[...]
        if norm:
            ms = jnp.mean(x * x, axis=-1, keepdims=True)
            x = x * lax.rsqrt(ms + EPS) * g_ref[...]
        hb = x.astype(BF16)
        h_hi[...] = hb
        if split:
            h_lo[...] = (x - hb.astype(F32)).astype(BF16)
        if emit_h:
            hout_ref[...] = hb

    acc = jnp.dot(h_hi[...], w_hi_ref[...], preferred_element_type=F32)
    if split:
        acc += jnp.dot(h_hi[...], w_lo_ref[...], preferred_element_type=F32)
        acc += jnp.dot(h_lo[...], w_hi_ref[...], preferred_element_type=F32)
    o_ref[...] = acc


def _project(x, g, w_hi, w_lo=None, *, norm=True, emit_h=False, tm, tn):
    t, d = x.shape
    n = w_hi.shape[1]
    split = w_lo is not None
    in_specs = [pl.BlockSpec((tm, d), lambda i, j: (i, 0)),
                pl.BlockSpec((1, d), lambda i, j: (0, 0)),
                pl.BlockSpec((d, tn), lambda i, j: (0, j))]
    args = [x, g.reshape(1, d), w_hi]
    if split:
        in_specs.append(pl.BlockSpec((d, tn), lambda i, j: (0, j)))
        args.append(w_lo)
    out_shape = [jax.ShapeDtypeStruct((t, n), F32)]
    out_specs = [pl.BlockSpec((tm, tn), lambda i, j: (i, j))]
    if emit_h:
        out_shape.append(jax.ShapeDtypeStruct((t, d), BF16))
        out_specs.append(pl.BlockSpec((tm, d), lambda i, j: (i, 0)))
    scratch = [pltpu.VMEM((tm, d), BF16)]
    if split:
        scratch.append(pltpu.VMEM((tm, d), BF16))
    res = pl.pallas_call(
        functools.partial(_proj_kernel, norm=norm, split=split, emit_h=emit_h),
        grid=(t // tm, n // tn),
        in_specs=in_specs, out_specs=out_specs, out_shape=out_shape,
        scratch_shapes=scratch,
        compiler_params=_cparams(("parallel", "arbitrary"), 48),
        name="proj_split" if split else "proj",
    )(*args)
    return res if emit_h else res[0]


def _conv_prompt_kernel(cb_ref, cc_ref, cx_ref, w_ref, y_ref, st_ref, u_scr):
    t = cb_ref.shape[0]
    u = cc_ref[...] * cx_ref[...]
    u_scr[0:SUBLANES, :] = jnp.zeros((SUBLANES, CONV_CH), F32)
    u_scr[SUBLANES:SUBLANES + t, :] = u
    w = w_ref[...]
    y = w[0:1, :] * u_scr[SUBLANES - 2:SUBLANES - 2 + t, :]
    y = y + w[1:2, :] * u_scr[SUBLANES - 1:SUBLANES - 1 + t, :]
    y = y + w[2:3, :] * u
    y_ref[...] = cb_ref[...] * y
    st_ref[...] = u_scr[SUBLANES + t - 2:SUBLANES + t, :]


def _conv_prompt(proj, conv_w):
    return pl.pallas_call(
        _conv_prompt_kernel,
        grid=(BATCH,),
        in_specs=[pl.BlockSpec((SEQ, CONV_CH), lambda b: (b, 0)),
                  pl.BlockSpec((SEQ, CONV_CH), lambda b: (b, 1)),
                  pl.BlockSpec((SEQ, CONV_CH), lambda b: (b, 2)),
                  pl.BlockSpec((CONV_WIDTH, CONV_CH), lambda b: (0, 0))],
        out_specs=[pl.BlockSpec((SEQ, CONV_CH), lambda b: (b, 0)),
                   pl.BlockSpec((None, CONV_WIDTH - 1, CONV_CH), lambda b: (b, 0, 0))],
        out_shape=[jax.ShapeDtypeStruct((N_PROMPT, CONV_CH), F32),
                   jax.ShapeDtypeStruct((BATCH, CONV_WIDTH - 1, CONV_CH), F32)],
        scratch_shapes=[pltpu.VMEM((SEQ + SUBLANES, CONV_CH), F32)],
        compiler_params=_cparams(("parallel",), 48),
        name="conv_prompt",
    )(proj, proj, proj, conv_w)


def _conv_sample_kernel(cb_ref, cc_ref, cx_ref, w_ref, st_in_ref, y_ref, st_ref, u_scr):
    w = w_ref[...]
    for s in range(DEC_BATCH):
        r = s * DEC_SEQ
        u_scr[0:2, :] = st_in_ref[s]
        u_scr[2:2 + DEC_SEQ, :] = cc_ref[r:r + DEC_SEQ, :] * cx_ref[r:r + DEC_SEQ, :]
        y = w[0:1, :] * u_scr[0:DEC_SEQ, :]
        y = y + w[1:2, :] * u_scr[1:1 + DEC_SEQ, :]
        y = y + w[2:3, :] * u_scr[2:2 + DEC_SEQ, :]
        y_ref[r:r + DEC_SEQ, :] = cb_ref[r:r + DEC_SEQ, :] * y
        st_ref[s] = u_scr[DEC_SEQ:DEC_SEQ + 2, :]


def _conv_sample(proj, conv_w, state):
    return pl.pallas_call(
        _conv_sample_kernel,
        grid=(1,),
        in_specs=[pl.BlockSpec((N_SAMPLE, CONV_CH), lambda i: (0, 0)),
                  pl.BlockSpec((N_SAMPLE, CONV_CH), lambda i: (0, 1)),
                  pl.BlockSpec((N_SAMPLE, CONV_CH), lambda i: (0, 2)),
                  pl.BlockSpec((CONV_WIDTH, CONV_CH), lambda i: (0, 0)),
                  pl.BlockSpec((DEC_BATCH, CONV_WIDTH - 1, CONV_CH), lambda i: (0, 0, 0))],
        out_specs=[pl.BlockSpec((N_SAMPLE, CONV_CH), lambda i: (0, 0)),
                   pl.BlockSpec((DEC_BATCH, CONV_WIDTH - 1, CONV_CH), lambda i: (0, 0, 0))],
        out_shape=[jax.ShapeDtypeStruct((N_SAMPLE, CONV_CH), F32),
                   jax.ShapeDtypeStruct((DEC_BATCH, CONV_WIDTH - 1, CONV_CH), F32)],
        scratch_shapes=[pltpu.VMEM((SUBLANES, CONV_CH), F32)],
        name="conv_sample",
    )(proj, proj, proj, conv_w, state)


def _rank_lt(gate, n_blocks, k, axis):
    block_id = lax.broadcasted_iota(jnp.int32, gate.shape, axis)
    rank = jnp.zeros(gate.shape, jnp.int32)
    for n2 in range(n_blocks):
        other = gate[n2:n2 + 1, :] if axis == 0 else gate[:, n2:n2 + 1]
        beats = jnp.where(other > gate, 1, jnp.where(other == gate, jnp.where(block_id > n2, 1, 0), 0))
        rank = rank + beats
    return jnp.where(rank < k, 1, 0)


def _for_past_block_pairs(i, update):
    for a in range(0, N_PROMPT_BLOCKS - 1, 2):
        b = a + 1
        if b < N_PROMPT_BLOCKS - 1:
            pl.when(b < i)(functools.partial(update, (a, b)))
        pl.when(b == i)(functools.partial(update, (a,)))


def _stage_kv(k_ref, v_ref, kb_scr, vt_scr):
    kb_scr[...] = k_ref[...].astype(BF16)
    for n in range(N_PROMPT_BLOCKS):
        vt_scr[n] = v_ref[n * MOBA_BLOCK:(n + 1) * MOBA_BLOCK, :].T.astype(BF16)


def _moba_prompt_kernel(q_ref, k_ref, v_ref, o_ref, kb_scr, vt_scr, km_scr, m_scr, l_scr, acc_scr):
    i = pl.program_id(2)
    tq = MOBA_BLOCK
    scale = HEAD_DIM ** -0.5

    @pl.when(i == 0)
    def _():
        _stage_kv(k_ref, v_ref, kb_scr, vt_scr)
        km_scr[...] = (jnp.sum(k_ref[...].reshape(N_PROMPT_BLOCKS, MOBA_BLOCK, HEAD_DIM), axis=1)
                       * (1.0 / MOBA_BLOCK))

    q = q_ref[...]
    gate = lax.dot_general(km_scr[...], q, NT, precision=HI, preferred_element_type=F32)
    blk = lax.broadcasted_iota(jnp.int32, gate.shape, 0)
    gate = jnp.where(blk < i, gate, NEG)
    sel = _rank_lt(gate, N_PROMPT_BLOCKS, MOBA_TOPK, axis=0) * jnp.where(blk < i, 1, 0)
    qb = q.astype(BF16)

    own = pl.multiple_of(i * tq, tq)
    s = lax.dot_general(kb_scr[pl.ds(own, tq), :], qb, NT, preferred_element_type=F32) * scale
    key = lax.broadcasted_iota(jnp.int32, (tq, tq), 0)
    qry = lax.broadcasted_iota(jnp.int32, (tq, tq), 1)
    s = jnp.where(key <= qry, s, NEG)
    m = jnp.max(s, axis=0, keepdims=True)
    p = jnp.exp(s - m)
    m_scr[...] = m
    l_scr[...] = jnp.sum(p, axis=0, keepdims=True)
    acc_scr[...] = jnp.dot(vt_scr[i], p.astype(BF16), preferred_element_type=F32)

    def past_update(blocks):
        scores = []
        for n in blocks:
            sn = lax.dot_general(kb_scr[n * tq:(n + 1) * tq, :], qb, NT, preferred_element_type=F32) * scale
            scores.append(jnp.where(sel[n:n + 1, :] > 0, sn, NEG))
        m_old = m_scr[...]
        m_new = m_old
        for sn in scores:
            m_new = jnp.maximum(m_new, jnp.max(sn, axis=0, keepdims=True))
        alpha = jnp.exp(m_old - m_new)
        l = alpha * l_scr[...]
        acc = alpha * acc_scr[...]
        for n, sn in zip(blocks, scores):
            pn = jnp.exp(sn - m_new)
            l = l + jnp.sum(pn, axis=0, keepdims=True)
            acc = acc + jnp.dot(vt_scr[n], pn.astype(BF16), preferred_element_type=F32)
        l_scr[...] = l
        acc_scr[...] = acc
        m_scr[...] = m_new

    _for_past_block_pairs(i, past_update)
    o_ref[...] = (acc_scr[...] / l_scr[...]).T


def _moba_prompt(proj):
    tq = MOBA_BLOCK
    return pl.pallas_call(
        _moba_prompt_kernel,
        grid=(BATCH, MOBA_HEADS, N_PROMPT_BLOCKS),
        in_specs=[pl.BlockSpec((tq, HEAD_DIM), lambda b, h, i: (b * N_PROMPT_BLOCKS + i, COL_MQ + h)),
                  pl.BlockSpec((SEQ, HEAD_DIM), lambda b, h, i: (b, COL_MK + h)),
                  pl.BlockSpec((SEQ, HEAD_DIM), lambda b, h, i: (b, COL_MV + h))],
        out_specs=pl.BlockSpec((tq, HEAD_DIM), lambda b, h, i: (b * N_PROMPT_BLOCKS + i, h)),
        out_shape=jax.ShapeDtypeStruct((N_PROMPT, MOBA_HEADS * HEAD_DIM), F32),
        scratch_shapes=[pltpu.VMEM((SEQ, HEAD_DIM), BF16), pltpu.VMEM((N_PROMPT_BLOCKS, HEAD_DIM, tq), BF16),
                        pltpu.VMEM((N_PROMPT_BLOCKS, HEAD_DIM), F32),
                        pltpu.VMEM((1, tq), F32), pltpu.VMEM((1, tq), F32), pltpu.VMEM((HEAD_DIM, tq), F32)],
        compiler_params=_cparams(("parallel", "parallel", "arbitrary"), 32),
        name="moba_prompt",
    )(proj, proj, proj)


def _diff_lambda(lq1_ref, lk1_ref, lq2_ref, lk2_ref, lam_init):
    a = jnp.sum(lq1_ref[...] * lk1_ref[...], axis=-1, keepdims=True)
    b = jnp.sum(lq2_ref[...] * lk2_ref[...], axis=-1, keepdims=True)
    return jnp.exp(a) - jnp.exp(b) + lam_init


def _diff_finish(o1, o2, lam, g, lam_init, axis=-1):
    o = o1 - lam * o2
    ms = jnp.mean(o * o, axis=axis, keepdims=True)
    return o * lax.rsqrt(ms + EPS) * g * (1.0 - lam_init)


def _split_maps(q):
    lane = lax.broadcasted_iota(jnp.int32, q.shape, 1)
    return jnp.where(lane < DIFF_QK_DIM, q, 0.0), jnp.where(lane >= DIFF_QK_DIM, q, 0.0)


def _diff_prompt_kernel(lq1_ref, lk1_ref, lq2_ref, lk2_ref, g_ref, q_ref, k_ref, v_ref, o_ref,
                        kb_scr, vt_scr, m_scr, l_scr, acc_scr, *, lam_init):
    i = pl.program_id(2)
    tq = MOBA_BLOCK
    scale = DIFF_QK_DIM ** -0.5

    @pl.when(i == 0)
    def _():
        _stage_kv(k_ref, v_ref, kb_scr, vt_scr)

    q1, q2 = _split_maps(q_ref[...])
    qs = (q1.astype(BF16), q2.astype(BF16))
    own = pl.multiple_of(i * tq, tq)
    key = lax.broadcasted_iota(jnp.int32, (tq, tq), 0)
    qry = lax.broadcasted_iota(jnp.int32, (tq, tq), 1)
    for c in range(2):
        s = lax.dot_general(kb_scr[pl.ds(own, tq), :], qs[c], NT, preferred_element_type=F32) * scale
        s = jnp.where(key <= qry, s, NEG)
        m = jnp.max(s, axis=0, keepdims=True)
        p = jnp.exp(s - m)
        m_scr[c] = m
        l_scr[c] = jnp.sum(p, axis=0, keepdims=True)
        acc_scr[c] = jnp.dot(vt_scr[i], p.astype(BF16), preferred_element_type=F32)

    def past_update(blocks):
        for c in range(2):
            scores = [lax.dot_general(kb_scr[n * tq:(n + 1) * tq, :], qs[c], NT,
                                      preferred_element_type=F32) * scale for n in blocks]
            m_old = m_scr[c]
            m_new = m_old
            for sn in scores:
                m_new = jnp.maximum(m_new, jnp.max(sn, axis=0, keepdims=True))
            alpha = jnp.exp(m_old - m_new)
            l = alpha * l_scr[c]
            acc = alpha * acc_scr[c]
            for n, sn in zip(blocks, scores):
                pn = jnp.exp(sn - m_new)
                l = l + jnp.sum(pn, axis=0, keepdims=True)
                acc = acc + jnp.dot(vt_scr[n], pn.astype(BF16), preferred_element_type=F32)
            l_scr[c] = l
            acc_scr[c] = acc
            m_scr[c] = m_new

    _for_past_block_pairs(i, past_update)
    lam = _diff_lambda(lq1_ref, lk1_ref, lq2_ref, lk2_ref, lam_init)
    y = _diff_finish(acc_scr[0] / l_scr[0], acc_scr[1] / l_scr[1], lam, g_ref[...], lam_init, axis=0)
    o_ref[...] = y.T


def _diff_prompt(proj, lams, subln_g, lam_init):
    tq = MOBA_BLOCK
    lam_specs = [pl.BlockSpec((1, DIFF_QK_DIM), lambda b, h, i: (0, 0))] * 4
    return pl.pallas_call(
        functools.partial(_diff_prompt_kernel, lam_init=lam_init),
        grid=(BATCH, DIFF_HEADS, N_PROMPT_BLOCKS),
        in_specs=lam_specs + [
            pl.BlockSpec((HEAD_DIM, 1), lambda b, h, i: (0, 0)),
            pl.BlockSpec((tq, HEAD_DIM), lambda b, h, i: (b * N_PROMPT_BLOCKS + i, COL_DQ + h)),
            pl.BlockSpec((SEQ, HEAD_DIM), lambda b, h, i: (b, COL_DK + h)),
            pl.BlockSpec((SEQ, HEAD_DIM), lambda b, h, i: (b, COL_DV + h))],
        out_specs=pl.BlockSpec((tq, HEAD_DIM), lambda b, h, i: (b * N_PROMPT_BLOCKS + i, h)),
        out_shape=jax.ShapeDtypeStruct((N_PROMPT, DIFF_HEADS * HEAD_DIM), F32),
        scratch_shapes=[pltpu.VMEM((SEQ, HEAD_DIM), BF16), pltpu.VMEM((N_PROMPT_BLOCKS, HEAD_DIM, tq), BF16),
                        pltpu.VMEM((2, 1, tq), F32), pltpu.VMEM((2, 1, tq), F32),
                        pltpu.VMEM((2, HEAD_DIM, tq), F32)],
        compiler_params=_cparams(("parallel", "parallel", "arbitrary"), 32),
        name="diff_prompt",
    )(*lams, subln_g.reshape(HEAD_DIM, 1), proj, proj, proj)


def _page_specs(layer, pinned_phase=None):
    specs = []
    for r in range(PAGES_PER_STEP):
        if pinned_phase is None:
            def imap(b, s, pt, r=r):
                return (layer, pt[b, s * PAGES_PER_STEP + r], 0, 0, 0)
        elif pinned_phase == 1:
            def imap(b, ph, s, pt, r=r):
                st = jnp.where(ph == 0, s, N_PAGE_STEPS - 1)
                return (layer, pt[b, st * PAGES_PER_STEP + r], 0, 0, 0)
        else:
            def imap(b, ph, s, pt, r=r):
                st = jnp.where(ph == 0, 0, s)
                return (layer, pt[b, st * PAGES_PER_STEP + r], 0, 0, 0)
        specs.append(pl.BlockSpec((None, None, MOBA_HEADS, PAGE_SIZE, HEAD_DIM), imap))
    return specs


def _head_major(cache):
    return jnp.transpose(cache, (0, 1, 3, 2, 4))


def _load_pages(refs, h):
    return jnp.concatenate([r[h] for r in refs], axis=0)


def _new_key_mask():
    r = lax.broadcasted_iota(jnp.int32, (SUBLANES, PAGE_SIZE), 0)
    j = lax.broadcasted_iota(jnp.int32, (SUBLANES, PAGE_SIZE), 1)
    return (j <= (r & (DEC_SEQ - 1))) & (j < DEC_SEQ)


def _stage_new_kv(kn_ref, vn_ref, knp_scr, vnp_scr, h):
    knp_scr[...] = jnp.zeros(knp_scr.shape, F32)
    vnp_scr[...] = jnp.zeros(vnp_scr.shape, F32)
    knp_scr[0:DEC_SEQ, :] = kn_ref[:, h * HEAD_DIM:(h + 1) * HEAD_DIM]
    vnp_scr[0:DEC_SEQ, :] = vn_ref[:, h * HEAD_DIM:(h + 1) * HEAD_DIM]


def _diff_sample_kernel(pt_ref, lq1_ref, lk1_ref, lq2_ref, lk2_ref, g_ref, q_ref, kn_ref, vn_ref, *rest,
                        lam_init):
    k_refs = rest[:PAGES_PER_STEP]
    v_refs = rest[PAGES_PER_STEP:2 * PAGES_PER_STEP]
    o_ref = rest[2 * PAGES_PER_STEP]
    q8_scr, knp_scr, vnp_scr, m_scr, l_scr, acc_scr = rest[2 * PAGES_PER_STEP + 1:]
    step = pl.program_id(1)
    scale = DIFF_QK_DIM ** -0.5

    @pl.when(step == 0)
    def _():
        for h in range(DIFF_HEADS):
            q1, q2 = _split_maps(q_ref[:, h * HEAD_DIM:(h + 1) * HEAD_DIM])
            q8_scr[h, 0:DEC_SEQ, :] = q1
            q8_scr[h, DEC_SEQ:2 * DEC_SEQ, :] = q2
        m_scr[...] = jnp.full(m_scr.shape, -jnp.inf, F32)
        l_scr[...] = jnp.zeros(l_scr.shape, F32)
        acc_scr[...] = jnp.zeros(acc_scr.shape, F32)

    def update(h, s, vb):
        m_old = m_scr[h]
        m_new = jnp.maximum(m_old, jnp.max(s, axis=-1, keepdims=True))
        alpha = jnp.exp(m_old - m_new)
        p = jnp.exp(s - m_new)
        l_scr[h] = alpha * l_scr[h] + jnp.sum(p, axis=-1, keepdims=True)
        acc_scr[h] = alpha * acc_scr[h] + jnp.dot(p.astype(BF16), vb, preferred_element_type=F32)
        m_scr[h] = m_new

    for h in range(DIFF_HEADS):
        q8 = q8_scr[h].astype(BF16)
        kb = _load_pages(k_refs, h).astype(BF16)
        vb = _load_pages(v_refs, h).astype(BF16)
        update(h, lax.dot_general(q8, kb, NT, preferred_element_type=F32) * scale, vb)

    @pl.when(step == N_PAGE_STEPS - 1)
    def _():
        lam = _diff_lambda(lq1_ref, lk1_ref, lq2_ref, lk2_ref, lam_init)
        for h in range(DIFF_HEADS):
            q8 = q8_scr[h].astype(BF16)
            _stage_new_kv(kn_ref, vn_ref, knp_scr, vnp_scr, h)
            sn = lax.dot_general(q8, knp_scr[...].astype(BF16), NT, preferred_element_type=F32) * scale
            update(h, jnp.where(_new_key_mask(), sn, NEG), vnp_scr[...].astype(BF16))
            o = acc_scr[h] / l_scr[h]
            o_ref[:, h * HEAD_DIM:(h + 1) * HEAD_DIM] = _diff_finish(
                o[0:DEC_SEQ, :], o[DEC_SEQ:2 * DEC_SEQ, :], lam, g_ref[...], lam_init)


def _tok_spec(c0):
    width = MOBA_HEADS * HEAD_DIM
    return pl.BlockSpec((None, DEC_SEQ, width), lambda b, *_: (b, 0, c0 // MOBA_HEADS))


def _diff_sample(proj3, cache_k, cache_v, page_table, layer, lams, subln_g, lam_init):
    lam_specs = [pl.BlockSpec((1, DIFF_QK_DIM), lambda b, s, pt: (0, 0))] * 4
    grid_spec = pltpu.PrefetchScalarGridSpec(
        num_scalar_prefetch=1,
        grid=(DEC_BATCH, N_PAGE_STEPS),
        in_specs=lam_specs + [pl.BlockSpec((1, HEAD_DIM), lambda b, s, pt: (0, 0)),
                              _tok_spec(COL_DQ), _tok_spec(COL_DK), _tok_spec(COL_DV)]
        + _page_specs(layer) + _page_specs(layer),
        out_specs=pl.BlockSpec((None, DEC_SEQ, DIFF_HEADS * HEAD_DIM), lambda b, s, pt: (b, 0, 0)),
        scratch_shapes=[pltpu.VMEM((DIFF_HEADS, SUBLANES, HEAD_DIM), F32),
                        pltpu.VMEM((PAGE_SIZE, HEAD_DIM), F32), pltpu.VMEM((PAGE_SIZE, HEAD_DIM), F32),
                        pltpu.VMEM((DIFF_HEADS, SUBLANES, 1), F32), pltpu.VMEM((DIFF_HEADS, SUBLANES, 1), F32),
                        pltpu.VMEM((DIFF_HEADS, SUBLANES, HEAD_DIM), F32)])
    return pl.pallas_call(
        functools.partial(_diff_sample_kernel, lam_init=lam_init),
        grid_spec=grid_spec,
        out_shape=jax.ShapeDtypeStruct((DEC_BATCH, DEC_SEQ, DIFF_HEADS * HEAD_DIM), F32),
        compiler_params=_cparams(("parallel", "arbitrary"), 56),
        name="diff_sample",
    )(page_table, *lams, subln_g.reshape(1, HEAD_DIM), proj3, proj3, proj3,
      *([cache_k] * PAGES_PER_STEP), *([cache_v] * PAGES_PER_STEP))


def _moba_sample_kernel(pt_ref, q_ref, kn_ref, vn_ref, *rest):
    k_refs = rest[:PAGES_PER_STEP]
    v_refs = rest[PAGES_PER_STEP:2 * PAGES_PER_STEP]
    o_ref = rest[2 * PAGES_PER_STEP]
    q8_scr, knp_scr, vnp_scr, s_scr, km_scr, p_scr, l_scr, acc_scr = rest[2 * PAGES_PER_STEP + 1:]
    phase = pl.program_id(1)
    step = pl.program_id(2)
    scale = HEAD_DIM ** -0.5

    @pl.when((phase == 0) & (step == 0))
    def _():
        q8_scr[...] = jnp.zeros(q8_scr.shape, F32)
        for h in range(MOBA_HEADS):
            q8_scr[h, 0:DEC_SEQ, :] = q_ref[:, h * HEAD_DIM:(h + 1) * HEAD_DIM]

    @pl.when(phase == 0)
    def _():
        for h in range(MOBA_HEADS):
            k = _load_pages(k_refs, h)
            km_scr[h, step] = (jnp.sum(k.reshape(BLOCKS_PER_STEP, MOBA_BLOCK, HEAD_DIM), axis=1)
                               * (1.0 / MOBA_BLOCK))
            s_scr[h, step] = lax.dot_general(q8_scr[h].astype(BF16), k.astype(BF16), NT,
                                             preferred_element_type=F32) * scale

    @pl.when((phase == 1) & (step == 0))
    def _():
        for h in range(MOBA_HEADS):
            q8 = q8_scr[h]
            km = km_scr[h].reshape(N_PAST_BLOCKS, HEAD_DIM)
            gate = lax.dot_general(q8, km, NT, precision=HI, preferred_element_type=F32)
            sel = _rank_lt(gate, N_PAST_BLOCKS, MOBA_TOPK, axis=1)
            _stage_new_kv(kn_ref, vn_ref, knp_scr, vnp_scr, h)
            sn = lax.dot_general(q8.astype(BF16), knp_scr[...].astype(BF16), NT,
                                 preferred_element_type=F32) * scale
            sn = jnp.where(_new_key_mask(), sn, NEG)
            m = jnp.max(sn, axis=-1, keepdims=True)

            def block(n, h=h, sel=sel):
                st, jb = divmod(n, BLOCKS_PER_STEP)
                return jnp.where(sel[:, n:n + 1] > 0,
                                 s_scr[h, st, :, jb * MOBA_BLOCK:(jb + 1) * MOBA_BLOCK], NEG)

            for n in range(N_PAST_BLOCKS):
                m = jnp.maximum(m, jnp.max(block(n), axis=-1, keepdims=True))
            pn = jnp.exp(sn - m)
            l = jnp.sum(pn, axis=-1, keepdims=True)
            for n in range(N_PAST_BLOCKS):
                st, jb = divmod(n, BLOCKS_PER_STEP)
                pb = jnp.exp(block(n) - m)
                l = l + jnp.sum(pb, axis=-1, keepdims=True)
                p_scr[h, st, :, jb * MOBA_BLOCK:(jb + 1) * MOBA_BLOCK] = pb
            l_scr[h] = l
            acc_scr[h] = jnp.dot(pn.astype(BF16), vnp_scr[...].astype(BF16), preferred_element_type=F32)

    @pl.when(phase == 1)
    def _():
        for h in range(MOBA_HEADS):
            vb = _load_pages(v_refs, h).astype(BF16)
            acc_scr[h] += jnp.dot(p_scr[h, step].astype(BF16), vb, preferred_element_type=F32)

    @pl.when((phase == 1) & (step == N_PAGE_STEPS - 1))
    def _():
        for h in range(MOBA_HEADS):
            o = acc_scr[h] / l_scr[h]
            o_ref[:, h * HEAD_DIM:(h + 1) * HEAD_DIM] = o[0:DEC_SEQ, :]


def _moba_sample(proj3, cache_k, cache_v, page_table, layer):
    grid_spec = pltpu.PrefetchScalarGridSpec(
        num_scalar_prefetch=1,
        grid=(DEC_BATCH, 2, N_PAGE_STEPS),
        in_specs=[_tok_spec(COL_MQ), _tok_spec(COL_MK), _tok_spec(COL_MV)]
        + _page_specs(layer, pinned_phase=1) + _page_specs(layer, pinned_phase=0),
        out_specs=pl.BlockSpec((None, DEC_SEQ, MOBA_HEADS * HEAD_DIM), lambda b, ph, s, pt: (b, 0, 0)),
        scratch_shapes=[pltpu.VMEM((MOBA_HEADS, SUBLANES, HEAD_DIM), F32),
                        pltpu.VMEM((PAGE_SIZE, HEAD_DIM), F32), pltpu.VMEM((PAGE_SIZE, HEAD_DIM), F32),
                        pltpu.VMEM((MOBA_HEADS, N_PAGE_STEPS, SUBLANES, KEYS_PER_STEP), F32),
                        pltpu.VMEM((MOBA_HEADS, N_PAGE_STEPS, BLOCKS_PER_STEP, HEAD_DIM), F32),
                        pltpu.VMEM((MOBA_HEADS, N_PAGE_STEPS, SUBLANES, KEYS_PER_STEP), F32),
                        pltpu.VMEM((MOBA_HEADS, SUBLANES, 1), F32),
                        pltpu.VMEM((MOBA_HEADS, SUBLANES, HEAD_DIM), F32)])
    return pl.pallas_call(
        _moba_sample_kernel,
        grid_spec=grid_spec,
        out_shape=jax.ShapeDtypeStruct((DEC_BATCH, DEC_SEQ, MOBA_HEADS * HEAD_DIM), F32),
        compiler_params=_cparams(("parallel", "arbitrary", "arbitrary"), 56),
        name="moba_sample",
    )(page_table, proj3, proj3, proj3, *([cache_k] * PAGES_PER_STEP), *([cache_v] * PAGES_PER_STEP))


def _out_proj_kernel(x_ref, c_ref, m_ref, d_ref, wc_ref, wm_ref, wd_ref, o_ref):
    acc = jnp.dot(c_ref[...].astype(BF16), wc_ref[...], preferred_element_type=F32)
    acc += jnp.dot(m_ref[...].astype(BF16), wm_ref[...], preferred_element_type=F32)
    acc += jnp.dot(d_ref[...].astype(BF16), wd_ref[...], preferred_element_type=F32)
    o_ref[...] = x_ref[...] + acc


def _out_proj(x, conv_o, moba_o, diff_o, wc, wm, wd, tm):
    t = x.shape[0]
    row = lambda w: pl.BlockSpec((tm, w), lambda i: (i, 0))
    full = lambda a: pl.BlockSpec(a.shape, lambda i: (0, 0))
    return pl.pallas_call(
        _out_proj_kernel,
        grid=(t // tm,),
        in_specs=[row(D_MODEL), row(conv_o.shape[1]), row(moba_o.shape[1]), row(diff_o.shape[1]),
                  full(wc), full(wm), full(wd)],
        out_specs=row(D_MODEL),
        out_shape=jax.ShapeDtypeStruct((t, D_MODEL), F32),
        compiler_params=_cparams(("parallel",), 48),
        name="out_proj",
    )(x, conv_o, moba_o, diff_o, wc, wm, wd)


def _top16_rows(s, row_id, id_bound):
    t = s.shape[1]
    slot = lax.broadcasted_iota(jnp.int32, (PEER_TOPK, t), 0)

    def body(k, carry):
        s, vals, idxs = carry
        m = jnp.max(s, axis=0, keepdims=True)
        idx = jnp.min(jnp.where(s == m, row_id, id_bound), axis=0, keepdims=True)
        vals = jnp.where(slot == k, m, vals)
        idxs = jnp.where(slot == k, idx, idxs)
        s = jnp.where(row_id == idx, -jnp.inf, s)
        return s, vals, idxs

    _, vals, idxs = lax.fori_loop(0, PEER_TOPK, body,
                                  (s, jnp.zeros((PEER_TOPK, t), F32), jnp.zeros((PEER_TOPK, t), jnp.int32)))
    return vals, idxs


def _route_kernel(q_ref, keys_ref, i1_ref, i2_ref, g_ref):
    t = q_ref.shape[0]
    key_id = lax.broadcasted_iota(jnp.int32, (PEER_N_KEYS, t), 0)
    r16 = lax.broadcasted_iota(jnp.int32, (PEER_TOPK, t), 0)
    r8 = lax.broadcasted_iota(jnp.int32, (SUBLANES, t), 0)
    pair_id = jnp.concatenate([r16] + [k1 * PEER_TOPK + r8 for k1 in range(1, SUBLANES)]
                              + [(r8 + SUBLANES) * PEER_TOPK], axis=0)
    i1_parts, i2_parts, g_parts = [], [], []
    for h in range(PEER_HEADS):
        tops = []
        for p in range(2):
            c0 = (h * 2 + p) * PEER_N_KEYS
            q_hp = q_ref[:, c0:c0 + PEER_N_KEYS]
            s_t = lax.dot_general(keys_ref[h, p], q_hp, NT, precision=HI, preferred_element_type=F32)
            tops.append(_top16_rows(s_t, key_id, PEER_N_KEYS))
        (v0, x0), (v1, x1) = tops
        cand = jnp.concatenate([v0[0:1, :] + v1]
                               + [v0[k1:k1 + 1, :] + v1[0:SUBLANES, :] for k1 in range(1, SUBLANES)]
                               + [v0[SUBLANES:PEER_TOPK, :] + v1[0:1, :]], axis=0)
        best, pos = _top16_rows(cand, pair_id, PEER_TOPK * PEER_TOPK)
        k1 = pos >> 4
        k2 = pos & (PEER_TOPK - 1)
        a = jnp.zeros(pos.shape, jnp.int32)
        b = jnp.zeros(pos.shape, jnp.int32)
        for k in range(PEER_TOPK):
            a = jnp.where(k1 == k, x0[k:k + 1, :], a)
            b = jnp.where(k2 == k, x1[k:k + 1, :], b)
        e = jnp.exp(best - jnp.max(best, axis=0, keepdims=True))
        g_parts.append(e / jnp.sum(e, axis=0, keepdims=True))
        i1_parts.append(a)
        i2_parts.append(b)
    i1_ref[...] = jnp.concatenate(i1_parts, axis=0).T
    i2_ref[...] = jnp.concatenate(i2_parts, axis=0).T
    g_ref[...] = jnp.concatenate(g_parts, axis=0).T


def _route(q, keys):
    t = q.shape[0]
    tb = min(4 * LANES, t)
    slots = PEER_HEADS * PEER_TOPK
    spec = pl.BlockSpec((tb, slots), lambda i: (i, 0))
    return pl.pallas_call(
        _route_kernel,
        grid=(t // tb,),
        in_specs=[pl.BlockSpec((tb, q.shape[1]), lambda i: (i, 0)),
                  pl.BlockSpec(keys.shape, lambda i: (0, 0, 0, 0))],
        out_specs=[spec, spec, spec],
        out_shape=[jax.ShapeDtypeStruct((t, slots), jnp.int32), jax.ShapeDtypeStruct((t, slots), jnp.int32),
                   jax.ShapeDtypeStruct((t, slots), F32)],
        compiler_params=_cparams(("parallel",), 32),
        name="peer_route",
    )(q, keys)


def _expert_kernel(x_ref, h_ref, i1_ref, i2_ref, g_ref, gf_ref, u_ref, v_ref, o_ref, w_scr, acc_scr, *,
                   final_norm):
    c = pl.program_id(1)
    tb = x_ref.shape[0]
    n_sub = PEER_EXPERT_CHUNK // PEER_N_KEYS
    w_stride = tb + W_PAD

    @pl.when(c == 0)
    def _():
        sub = lax.broadcasted_iota(jnp.int32, (PEER_N_KEYS, PEER_N_KEYS), 0)

        def build(t, carry):
            i1_row = i1_ref[pl.ds(t, 1), :]
            i2_row = i2_ref[pl.ds(t, 1), :]
            g_row = g_ref[pl.ds(t, 1), :]
            p_t = jnp.where(sub == i1_row, g_row, 0.0).astype(BF16)
            q_t = jnp.where(sub == i2_row, 1.0, 0.0).astype(BF16)
            w_t = lax.dot_general(p_t, q_t, NT, preferred_element_type=F32)
            w_scr[pl.ds(t, PEER_N_KEYS, stride=w_stride), :] = w_t
            return carry

        lax.fori_loop(0, tb, build, 0, unroll=16)
        acc_scr[...] = jnp.zeros(acc_scr.shape, F32)

    act = lax.dot_general(h_ref[...], u_ref[...], NT, preferred_element_type=F32)
    gel = 0.5 * act * (1.0 + lax.erf(act * math.sqrt(0.5)))
    parts = []
    for a in range(n_sub):
        w_a = w_scr[pl.ds(pl.multiple_of((c * n_sub + a) * w_stride, SUBLANES), tb), :]
        parts.append((w_a * gel[:, a * PEER_N_KEYS:(a + 1) * PEER_N_KEYS]).astype(BF16))
    wa = jnp.concatenate(parts, axis=1)
    acc_scr[...] += jnp.dot(wa, v_ref[...], preferred_element_type=F32)

    @pl.when(c == pl.num_programs(1) - 1)
    def _():
        y = x_ref[...] + acc_scr[...]
        if final_norm:
            ms = jnp.mean(y * y, axis=-1, keepdims=True)
            y = y * lax.rsqrt(ms + EPS) * gf_ref[...]
        o_ref[...] = y


def _experts(x, h, i1, i2, g, g_final, u, v, *, tb, final_norm):
    t = x.shape[0]
    slots = i1.shape[1]
    row = lambda w: pl.BlockSpec((tb, w), lambda i, c: (i, 0))
    chunk = pl.BlockSpec((PEER_EXPERT_CHUNK, D_MODEL), lambda i, c: (c, 0))
    return pl.pallas_call(
        functools.partial(_expert_kernel, final_norm=final_norm),
        grid=(t // tb, PEER_EXPERTS // PEER_EXPERT_CHUNK),
        in_specs=[row(D_MODEL), row(D_MODEL), row(slots), row(slots), row(slots),
                  pl.BlockSpec((1, D_MODEL), lambda i, c: (0, 0)), chunk, chunk],
        out_specs=row(D_MODEL),
        out_shape=jax.ShapeDtypeStruct((t, D_MODEL), F32),
        scratch_shapes=[pltpu.VMEM((PEER_N_KEYS * (tb + W_PAD), PEER_N_KEYS), F32),
                        pltpu.VMEM((tb, D_MODEL), F32)],
        compiler_params=_cparams(("parallel", "arbitrary"), 56),
        name="peer_experts",
    )(x, h, i1, i2, g, g_final.reshape(1, D_MODEL), u, v)


def _split_bf16(w):
    hi = w.astype(BF16)
    return hi, (w - hi.astype(F32)).astype(BF16)


def _layer_tail(x, conv_o, moba_o, diff_o, lw, g_final, *, tm, tb, final_norm):
    x1 = _out_proj(x, conv_o, moba_o, diff_o, lw["wo_c"], lw["wo_m"], lw["wo_d"], tm=min(tm, 256))
    q, h2 = _project(x1, lw["g_ffn"], lw["wq_hi"], lw["wq_lo"], norm=True, emit_h=True, tm=tm, tn=1024)
    i1, i2, g = _route(q, lw["keys"])
    return _experts(x1, h2, i1, i2, g, g_final, lw["u"], lw["v"], tb=tb, final_norm=final_norm)


def kernel(x_prompt, x_sample, cache_moba_k, cache_moba_v, cache_diff_k, cache_diff_v, state_conv, page_table,
           g_mix, w_in, conv_w, diff_lambda_q1, diff_lambda_k1, diff_lambda_q2, diff_lambda_k2, diff_subln_g,
           w_out, g_ffn, peer_w_query, peer_sub_keys, peer_expert_u, peer_expert_v, g_final):
    xp = x_prompt.reshape(N_PROMPT, D_MODEL)
    xs = x_sample.reshape(N_SAMPLE, D_MODEL)
    cache_moba_k, cache_moba_v, cache_diff_k, cache_diff_v = (
        _head_major(c) for c in (cache_moba_k, cache_moba_v, cache_diff_k, cache_diff_v))
    prompt_state, sample_state = [], []
    for l in range(DEPTH):
        lam_init = 0.8 - 0.6 * math.exp(-0.3 * l)
        final = l == DEPTH - 1
        wq_hi, wq_lo = _split_bf16(peer_w_query[l])
        wo = w_out[l].astype(BF16)
        lw = dict(wo_c=wo[:CONV_CH], wo_m=wo[CONV_CH:CONV_CH + MOBA_HEADS * HEAD_DIM],
                  wo_d=wo[CONV_CH + MOBA_HEADS * HEAD_DIM:], g_ffn=g_ffn[l], wq_hi=wq_hi, wq_lo=wq_lo,
                  keys=peer_sub_keys[l], u=peer_expert_u[l].astype(BF16), v=peer_expert_v[l].astype(BF16))
        w_in_b = w_in[l].astype(BF16)
        lams = [a[l].reshape(1, DIFF_QK_DIM) for a in
                (diff_lambda_q1, diff_lambda_k1, diff_lambda_q2, diff_lambda_k2)]

        proj = _project(xp, g_mix[l], w_in_b, norm=True, tm=512, tn=1024)
        conv_o, conv_new = _conv_prompt(proj, conv_w[l])
        moba_o = _moba_prompt(proj)
        diff_o = _diff_prompt(proj, lams, diff_subln_g[l], lam_init)
        xp = _layer_tail(xp, conv_o, moba_o, diff_o, lw, g_final, tm=512, tb=256, final_norm=final)
        prompt_state.append((proj, conv_new))

        proj_s = _project(xs, g_mix[l], w_in_b, norm=True, tm=N_SAMPLE, tn=1024)
        proj3 = proj_s.reshape(DEC_BATCH, DEC_SEQ, proj_s.shape[1])
        conv_o, conv_new = _conv_sample(proj_s, conv_w[l], state_conv[l])
        moba_o = _moba_sample(proj3, cache_moba_k, cache_moba_v, page_table, l)
        diff_o = _diff_sample(proj3, cache_diff_k, cache_diff_v, page_table, l, lams, diff_subln_g[l], lam_init)
        xs = _layer_tail(xs, conv_o, moba_o.reshape(N_SAMPLE, -1), diff_o.reshape(N_SAMPLE, -1), lw, g_final,
                         tm=N_SAMPLE, tb=N_SAMPLE, final_norm=final)
        sample_state.append((proj_s, conv_new))

    def states(group, b, t):
        cols = lambda c0: jnp.stack([p[:, c0 * LANES:(c0 + MOBA_HEADS) * LANES].reshape(b, t, MOBA_HEADS, HEAD_DIM)
                                     for p, _ in group])
        return cols(COL_MK), cols(COL_MV), cols(COL_DK), cols(COL_DV), jnp.stack([c for _, c in group])

    y_prompt = xp.reshape(BATCH, SEQ, D_MODEL)
    y_sample = xs.reshape(DEC_BATCH, DEC_SEQ, D_MODEL)
    return (y_prompt, y_sample) + states(prompt_state, BATCH, SEQ) + states(sample_state, DEC_BATCH, DEC_SEQ)
```

```python
import functools
import math

import jax
import jax.numpy as jnp
from jax import lax
from jax.experimental import pallas as pl
from jax.experimental.pallas import tpu as pltpu

D_MODEL = 2048
BATCH = 4
SEQ = 2048
DEPTH = 2
DEC_BATCH = 32
DEC_SEQ = 4
PAST_LEN = 8192
PAGE_SIZE = 128
HEAD_DIM = 128
CONV_CH = 512
CONV_WIDTH = 3
MOBA_HEADS = 6
MOBA_BLOCK = 256
MOBA_TOPK = 3
DIFF_HEADS = 6
DIFF_QK_DIM = 64
PEER_HEADS = 8
PEER_N_KEYS = 128
PEER_EXPERTS = PEER_N_KEYS * PEER_N_KEYS
PEER_TOPK = 16
EPS = 1e-6
NEG = -1e30

N_PAGES = PAST_LEN // PAGE_SIZE
N_SAMPLE = DEC_BATCH * DEC_SEQ
N_PROMPT = BATCH * SEQ
N_PROMPT_BLOCKS = SEQ // MOBA_BLOCK
N_PAST_BLOCKS = PAST_LEN // MOBA_BLOCK

COL_MQ, COL_MK, COL_MV = 12, 18, 24
COL_DQ, COL_DK, COL_DV = 30, 36, 42

LANES = 128
SUBLANES = 8
MIB = 1024 * 1024

PAGES_PER_STEP = 16
N_PAGE_STEPS = N_PAGES // PAGES_PER_STEP
KEYS_PER_STEP = PAGES_PER_STEP * PAGE_SIZE
BLOCKS_PER_STEP = KEYS_PER_STEP // MOBA_BLOCK

PEER_EXPERT_CHUNK = 1024
W_PAD = SUBLANES

F32 = jnp.float32
BF16 = jnp.bfloat16
HI = lax.Precision.HIGHEST
NT = (((1,), (1,)), ((), ()))


def _cparams(sem, vmem_mib):
    return pltpu.CompilerParams(dimension_semantics=sem, vmem_limit_bytes=vmem_mib * MIB)


STATE_WIDTH = MOBA_HEADS * HEAD_DIM
STATE_TILES = tuple(c // MOBA_HEADS for c in (COL_MK, COL_MV, COL_DK, COL_DV))


def _proj_kernel(x_ref, g_ref, w_ref, o_ref, *rest, emit_h, emit_states):
    rest = list(rest)
    hout_ref = rest.pop(0) if emit_h else None
    state_refs = [rest.pop(0) for _ in STATE_TILES] if emit_states else []
    h_scr = rest.pop(0)
    j = pl.program_id(1)

    @pl.when(j == 0)
    def _():
        x = x_ref[...]
        ms = jnp.mean(x * x, axis=-1, keepdims=True)
        hb = (x * lax.rsqrt(ms + EPS) * g_ref[...]).astype(BF16)
        h_scr[...] = hb
        if emit_h:
            hout_ref[...] = hb

    o_ref[...] = jnp.dot(h_scr[...], w_ref[...], preferred_element_type=F32)

    for tile, s_ref in zip(STATE_TILES, state_refs):
        @pl.when(j == tile)
        def _(s_ref=s_ref):
            for h in range(MOBA_HEADS):
                s_ref[h] = o_ref[:, h * HEAD_DIM:(h + 1) * HEAD_DIM]


def _project(x, g, w, *, emit_h=False, seq=None, tm, tn):
    t, d = x.shape
    n = w.shape[1]
    emit_states = seq is not None
    out_shape = [jax.ShapeDtypeStruct((t, n), F32)]
    out_specs = [pl.BlockSpec((tm, tn), lambda i, j: (i, j))]
    if emit_h:
        out_shape.append(jax.ShapeDtypeStruct((t, d), BF16))
        out_specs.append(pl.BlockSpec((tm, d), lambda i, j: (i, 0)))
    if emit_states:
        assert tn == STATE_WIDTH and seq % tm == 0
        per_seq = seq // tm
        for _ in STATE_TILES:
            out_shape.append(jax.ShapeDtypeStruct((t // seq, MOBA_HEADS, seq, HEAD_DIM), F32))
            out_specs.append(pl.BlockSpec((None, MOBA_HEADS, tm, HEAD_DIM),
                                          lambda i, j: (i // per_seq, 0, i % per_seq, 0)))
    res = pl.pallas_call(
        functools.partial(_proj_kernel, emit_h=emit_h, emit_states=emit_states),
        grid=(t // tm, n // tn),
        in_specs=[pl.BlockSpec((tm, d), lambda i, j: (i, 0)),
                  pl.BlockSpec((1, d), lambda i, j: (0, 0)),
                  pl.BlockSpec((d, tn), lambda i, j: (0, j))],
        out_specs=out_specs, out_shape=out_shape,
        scratch_shapes=[pltpu.VMEM((tm, d), BF16)],
        compiler_params=_cparams(("parallel", "arbitrary"), 48),
        name="proj",
    )(x, g.reshape(1, d), w)
    return res if len(res) > 1 else res[0]


def _conv_prompt_kernel(cb_ref, cc_ref, cx_ref, w_ref, y_ref, st_ref, u_scr):
    t = cb_ref.shape[0]
    u = cc_ref[...] * cx_ref[...]
    u_scr[0:SUBLANES, :] = jnp.zeros((SUBLANES, CONV_CH), F32)
    u_scr[SUBLANES:SUBLANES + t, :] = u
    w = w_ref[...]
    y = w[0:1, :] * u_scr[SUBLANES - 2:SUBLANES - 2 + t, :]
    y = y + w[1:2, :] * u_scr[SUBLANES - 1:SUBLANES - 1 + t, :]
    y = y + w[2:3, :] * u
    y_ref[...] = cb_ref[...] * y
    st_ref[...] = u_scr[SUBLANES + t - 2:SUBLANES + t, :]


def _conv_prompt(proj, conv_w):
    return pl.pallas_call(
        _conv_prompt_kernel,
        grid=(BATCH,),
        in_specs=[pl.BlockSpec((SEQ, CONV_CH), lambda b: (b, 0)),
                  pl.BlockSpec((SEQ, CONV_CH), lambda b: (b, 1)),
                  pl.BlockSpec((SEQ, CONV_CH), lambda b: (b, 2)),
                  pl.BlockSpec((CONV_WIDTH, CONV_CH), lambda b: (0, 0))],
        out_specs=[pl.BlockSpec((SEQ, CONV_CH), lambda b: (b, 0)),
                   pl.BlockSpec((None, CONV_WIDTH - 1, CONV_CH), lambda b: (b, 0, 0))],
        out_shape=[jax.ShapeDtypeStruct((N_PROMPT, CONV_CH), F32),
                   jax.ShapeDtypeStruct((BATCH, CONV_WIDTH - 1, CONV_CH), F32)],
        scratch_shapes=[pltpu.VMEM((SEQ + SUBLANES, CONV_CH), F32)],
        compiler_params=_cparams(("parallel",), 48),
        name="conv_prompt",
    )(proj, proj, proj, conv_w)


def _conv_sample_kernel(cb_ref, cc_ref, cx_ref, w_ref, st_in_ref, y_ref, st_ref, u_scr):
    w = w_ref[...]
    for s in range(DEC_BATCH):
        r = s * DEC_SEQ
        u_scr[0:2, :] = st_in_ref[s]
        u_scr[2:2 + DEC_SEQ, :] = cc_ref[r:r + DEC_SEQ, :] * cx_ref[r:r + DEC_SEQ, :]
        y = w[0:1, :] * u_scr[0:DEC_SEQ, :]
        y = y + w[1:2, :] * u_scr[1:1 + DEC_SEQ, :]
        y = y + w[2:3, :] * u_scr[2:2 + DEC_SEQ, :]
        y_ref[r:r + DEC_SEQ, :] = cb_ref[r:r + DEC_SEQ, :] * y
        st_ref[s] = u_scr[DEC_SEQ:DEC_SEQ + 2, :]


def _conv_sample(proj, conv_w, state):
    return pl.pallas_call(
        _conv_sample_kernel,
        grid=(1,),
        in_specs=[pl.BlockSpec((N_SAMPLE, CONV_CH), lambda i: (0, 0)),
                  pl.BlockSpec((N_SAMPLE, CONV_CH), lambda i: (0, 1)),
                  pl.BlockSpec((N_SAMPLE, CONV_CH), lambda i: (0, 2)),
                  pl.BlockSpec((CONV_WIDTH, CONV_CH), lambda i: (0, 0)),
                  pl.BlockSpec((DEC_BATCH, CONV_WIDTH - 1, CONV_CH), lambda i: (0, 0, 0))],
        out_specs=[pl.BlockSpec((N_SAMPLE, CONV_CH), lambda i: (0, 0)),
                   pl.BlockSpec((DEC_BATCH, CONV_WIDTH - 1, CONV_CH), lambda i: (0, 0, 0))],
        out_shape=[jax.ShapeDtypeStruct((N_SAMPLE, CONV_CH), F32),
                   jax.ShapeDtypeStruct((DEC_BATCH, CONV_WIDTH - 1, CONV_CH), F32)],
        scratch_shapes=[pltpu.VMEM((SUBLANES, CONV_CH), F32)],
        name="conv_sample",
    )(proj, proj, proj, conv_w, state)


def _rank_lt(gate, n_blocks, k, axis):
    block_id = lax.broadcasted_iota(jnp.int32, gate.shape, axis)
    rank = jnp.zeros(gate.shape, jnp.int32)
    for n2 in range(n_blocks):
        other = gate[n2:n2 + 1, :] if axis == 0 else gate[:, n2:n2 + 1]
        beats = jnp.where(other > gate, 1, jnp.where(other == gate, jnp.where(block_id > n2, 1, 0), 0))
        rank = rank + beats
    return jnp.where(rank < k, 1, 0)


def _for_past_block_pairs(i, update):
    for a in range(0, N_PROMPT_BLOCKS - 1, 2):
        b = a + 1
        if b < N_PROMPT_BLOCKS - 1:
            pl.when(b < i)(functools.partial(update, (a, b)))
        pl.when(b == i)(functools.partial(update, (a,)))


def _stage_kv(k_ref, v_ref, kb_scr, vt_scr):
    kb_scr[...] = k_ref[...].astype(BF16)
    for n in range(N_PROMPT_BLOCKS):
        vt_scr[n] = v_ref[n * MOBA_BLOCK:(n + 1) * MOBA_BLOCK, :].T.astype(BF16)


def _moba_prompt_kernel(q_ref, k_ref, v_ref, o_ref, kb_scr, vt_scr, km_scr, m_scr, l_scr, acc_scr):
    i = pl.program_id(2)
    tq = MOBA_BLOCK
    scale = HEAD_DIM ** -0.5

    @pl.when(i == 0)
    def _():
        _stage_kv(k_ref, v_ref, kb_scr, vt_scr)
        km_scr[...] = (jnp.sum(k_ref[...].reshape(N_PROMPT_BLOCKS, MOBA_BLOCK, HEAD_DIM), axis=1)
                       * (1.0 / MOBA_BLOCK))

    q = q_ref[...]
    gate = lax.dot_general(km_scr[...], q, NT, precision=HI, preferred_element_type=F32)
    blk = lax.broadcasted_iota(jnp.int32, gate.shape, 0)
    gate = jnp.where(blk < i, gate, NEG)
    sel = _rank_lt(gate, N_PROMPT_BLOCKS, MOBA_TOPK, axis=0) * jnp.where(blk < i, 1, 0)
    qb = q.astype(BF16)

    own = pl.multiple_of(i * tq, tq)
    s = lax.dot_general(kb_scr[pl.ds(own, tq), :], qb, NT, preferred_element_type=F32) * scale
    key = lax.broadcasted_iota(jnp.int32, (tq, tq), 0)
    qry = lax.broadcasted_iota(jnp.int32, (tq, tq), 1)
    s = jnp.where(key <= qry, s, NEG)
    m = jnp.max(s, axis=0, keepdims=True)
    p = jnp.exp(s - m)
    m_scr[...] = m
    l_scr[...] = jnp.sum(p, axis=0, keepdims=True)
    acc_scr[...] = jnp.dot(vt_scr[i], p.astype(BF16), preferred_element_type=F32)

    def past_update(blocks):
        scores = []
        for n in blocks:
            sn = lax.dot_general(kb_scr[n * tq:(n + 1) * tq, :], qb, NT, preferred_element_type=F32) * scale
            scores.append(jnp.where(sel[n:n + 1, :] > 0, sn, NEG))
        m_old = m_scr[...]
        m_new = m_old
        for sn in scores:
            m_new = jnp.maximum(m_new, jnp.max(sn, axis=0, keepdims=True))
        alpha = jnp.exp(m_old - m_new)
        l = alpha * l_scr[...]
        acc = alpha * acc_scr[...]
        for n, sn in zip(blocks, scores):
            pn = jnp.exp(sn - m_new)
            l = l + jnp.sum(pn, axis=0, keepdims=True)
            acc = acc + jnp.dot(vt_scr[n], pn.astype(BF16), preferred_element_type=F32)
        l_scr[...] = l
        acc_scr[...] = acc
        m_scr[...] = m_new

    _for_past_block_pairs(i, past_update)
    o_ref[...] = (acc_scr[...] / l_scr[...]).T


def _moba_prompt(proj):
    tq = MOBA_BLOCK
    return pl.pallas_call(
        _moba_prompt_kernel,
        grid=(BATCH, MOBA_HEADS, N_PROMPT_BLOCKS),
        in_specs=[pl.BlockSpec((tq, HEAD_DIM), lambda b, h, i: (b * N_PROMPT_BLOCKS + i, COL_MQ + h)),
                  pl.BlockSpec((SEQ, HEAD_DIM), lambda b, h, i: (b, COL_MK + h)),
                  pl.BlockSpec((SEQ, HEAD_DIM), lambda b, h, i: (b, COL_MV + h))],
        out_specs=pl.BlockSpec((tq, HEAD_DIM), lambda b, h, i: (b * N_PROMPT_BLOCKS + i, h)),
        out_shape=jax.ShapeDtypeStruct((N_PROMPT, MOBA_HEADS * HEAD_DIM), F32),
        scratch_shapes=[pltpu.VMEM((SEQ, HEAD_DIM), BF16), pltpu.VMEM((N_PROMPT_BLOCKS, HEAD_DIM, tq), BF16),
                        pltpu.VMEM((N_PROMPT_BLOCKS, HEAD_DIM), F32),
                        pltpu.VMEM((1, tq), F32), pltpu.VMEM((1, tq), F32), pltpu.VMEM((HEAD_DIM, tq), F32)],
        compiler_params=_cparams(("parallel", "parallel", "arbitrary"), 32),
        name="moba_prompt",
    )(proj, proj, proj)


def _diff_lambda(lq1_ref, lk1_ref, lq2_ref, lk2_ref, lam_init):
    a = jnp.sum(lq1_ref[...] * lk1_ref[...], axis=-1, keepdims=True)
    b = jnp.sum(lq2_ref[...] * lk2_ref[...], axis=-1, keepdims=True)
    return jnp.exp(a) - jnp.exp(b) + lam_init


def _diff_finish(o1, o2, lam, g, lam_init, axis=-1):
    o = o1 - lam * o2
    ms = jnp.mean(o * o, axis=axis, keepdims=True)
    return o * lax.rsqrt(ms + EPS) * g * (1.0 - lam_init)


def _split_maps(q):
    lane = lax.broadcasted_iota(jnp.int32, q.shape, 1)
    return jnp.where(lane < DIFF_QK_DIM, q, 0.0), jnp.where(lane >= DIFF_QK_DIM, q, 0.0)


def _diff_prompt_kernel(lq1_ref, lk1_ref, lq2_ref, lk2_ref, g_ref, q_ref, k_ref, v_ref, o_ref,
                        kb_scr, vt_scr, m_scr, l_scr, acc_scr, *, lam_init):
    i = pl.program_id(2)
    tq = MOBA_BLOCK
    scale = DIFF_QK_DIM ** -0.5

    @pl.when(i == 0)
    def _():
        _stage_kv(k_ref, v_ref, kb_scr, vt_scr)

    q1, q2 = _split_maps(q_ref[...])
    qs = (q1.astype(BF16), q2.astype(BF16))
    own = pl.multiple_of(i * tq, tq)
    key = lax.broadcasted_iota(jnp.int32, (tq, tq), 0)
    qry = lax.broadcasted_iota(jnp.int32, (tq, tq), 1)
    for c in range(2):
        s = lax.dot_general(kb_scr[pl.ds(own, tq), :], qs[c], NT, preferred_element_type=F32) * scale
        s = jnp.where(key <= qry, s, NEG)
        m = jnp.max(s, axis=0, keepdims=True)
        p = jnp.exp(s - m)
        m_scr[c] = m
        l_scr[c] = jnp.sum(p, axis=0, keepdims=True)
        acc_scr[c] = jnp.dot(vt_scr[i], p.astype(BF16), preferred_element_type=F32)

    def past_update(blocks):
        for c in range(2):
            scores = [lax.dot_general(kb_scr[n * tq:(n + 1) * tq, :], qs[c], NT,
                                      preferred_element_type=F32) * scale for n in blocks]
            m_old = m_scr[c]
            m_new = m_old
            for sn in scores:
                m_new = jnp.maximum(m_new, jnp.max(sn, axis=0, keepdims=True))
            alpha = jnp.exp(m_old - m_new)
            l = alpha * l_scr[c]
            acc = alpha * acc_scr[c]
            for n, sn in zip(blocks, scores):
                pn = jnp.exp(sn - m_new)
                l = l + jnp.sum(pn, axis=0, keepdims=True)
                acc = acc + jnp.dot(vt_scr[n], pn.astype(BF16), preferred_element_type=F32)
            l_scr[c] = l
            acc_scr[c] = acc
            m_scr[c] = m_new

    _for_past_block_pairs(i, past_update)
    lam = _diff_lambda(lq1_ref, lk1_ref, lq2_ref, lk2_ref, lam_init)
    y = _diff_finish(acc_scr[0] / l_scr[0], acc_scr[1] / l_scr[1], lam, g_ref[...], lam_init, axis=0)
    o_ref[...] = y.T


def _diff_prompt(proj, lams, subln_g, lam_init):
    tq = MOBA_BLOCK
    lam_specs = [pl.BlockSpec((1, DIFF_QK_DIM), lambda b, h, i: (0, 0))] * 4
    return pl.pallas_call(
        functools.partial(_diff_prompt_kernel, lam_init=lam_init),
        grid=(BATCH, DIFF_HEADS, N_PROMPT_BLOCKS),
        in_specs=lam_specs + [
            pl.BlockSpec((HEAD_DIM, 1), lambda b, h, i: (0, 0)),
            pl.BlockSpec((tq, HEAD_DIM), lambda b, h, i: (b * N_PROMPT_BLOCKS + i, COL_DQ + h)),
            pl.BlockSpec((SEQ, HEAD_DIM), lambda b, h, i: (b, COL_DK + h)),
            pl.BlockSpec((SEQ, HEAD_DIM), lambda b, h, i: (b, COL_DV + h))],
        out_specs=pl.BlockSpec((tq, HEAD_DIM), lambda b, h, i: (b * N_PROMPT_BLOCKS + i, h)),
        out_shape=jax.ShapeDtypeStruct((N_PROMPT, DIFF_HEADS * HEAD_DIM), F32),
        scratch_shapes=[pltpu.VMEM((SEQ, HEAD_DIM), BF16), pltpu.VMEM((N_PROMPT_BLOCKS, HEAD_DIM, tq), BF16),
                        pltpu.VMEM((2, 1, tq), F32), pltpu.VMEM((2, 1, tq), F32),
                        pltpu.VMEM((2, HEAD_DIM, tq), F32)],
        compiler_params=_cparams(("parallel", "parallel", "arbitrary"), 32),
        name="diff_prompt",
    )(*lams, subln_g.reshape(HEAD_DIM, 1), proj, proj, proj)


def _page_specs(layer, pinned_phase=None):
    specs = []
    for r in range(PAGES_PER_STEP):
        if pinned_phase is None:
            def imap(b, s, pt, r=r):
                return (layer, pt[b, s * PAGES_PER_STEP + r], 0, 0, 0)
        elif pinned_phase == 1:
            def imap(b, ph, s, pt, r=r):
                st = jnp.where(ph == 0, s, N_PAGE_STEPS - 1)
                return (layer, pt[b, st * PAGES_PER_STEP + r], 0, 0, 0)
        else:
            def imap(b, ph, s, pt, r=r):
                st = jnp.where(ph == 0, 0, s)
                return (layer, pt[b, st * PAGES_PER_STEP + r], 0, 0, 0)
        specs.append(pl.BlockSpec((None, None, MOBA_HEADS, PAGE_SIZE, HEAD_DIM), imap))
    return specs


def _head_major(cache):
    return jnp.transpose(cache, (0, 1, 3, 2, 4))


def _load_pages(refs, h):
    return jnp.concatenate([r[h] for r in refs], axis=0)


def _new_key_mask():
    r = lax.broadcasted_iota(jnp.int32, (SUBLANES, PAGE_SIZE), 0)
    j = lax.broadcasted_iota(jnp.int32, (SUBLANES, PAGE_SIZE), 1)
    return (j <= (r & (DEC_SEQ - 1))) & (j < DEC_SEQ)


def _stage_new_kv(kn_ref, vn_ref, knp_scr, vnp_scr, h):
    knp_scr[...] = jnp.zeros(knp_scr.shape, F32)
    vnp_scr[...] = jnp.zeros(vnp_scr.shape, F32)
    knp_scr[0:DEC_SEQ, :] = kn_ref[:, h * HEAD_DIM:(h + 1) * HEAD_DIM]
    vnp_scr[0:DEC_SEQ, :] = vn_ref[:, h * HEAD_DIM:(h + 1) * HEAD_DIM]


def _diff_sample_kernel(pt_ref, lq1_ref, lk1_ref, lq2_ref, lk2_ref, g_ref, q_ref, kn_ref, vn_ref, *rest,
                        lam_init):
    k_refs = rest[:PAGES_PER_STEP]
    v_refs = rest[PAGES_PER_STEP:2 * PAGES_PER_STEP]
    o_ref = rest[2 * PAGES_PER_STEP]
    q8_scr, knp_scr, vnp_scr, m_scr, l_scr, acc_scr = rest[2 * PAGES_PER_STEP + 1:]
    step = pl.program_id(1)
    scale = DIFF_QK_DIM ** -0.5

    @pl.when(step == 0)
    def _():
        for h in range(DIFF_HEADS):
            q1, q2 = _split_maps(q_ref[:, h * HEAD_DIM:(h + 1) * HEAD_DIM])
            q8_scr[h, 0:DEC_SEQ, :] = q1
            q8_scr[h, DEC_SEQ:2 * DEC_SEQ, :] = q2
        m_scr[...] = jnp.full(m_scr.shape, -jnp.inf, F32)
        l_scr[...] = jnp.zeros(l_scr.shape, F32)
        acc_scr[...] = jnp.zeros(acc_scr.shape, F32)

    def update(h, s, vb):
        m_old = m_scr[h]
        m_new = jnp.maximum(m_old, jnp.max(s, axis=-1, keepdims=True))
        alpha = jnp.exp(m_old - m_new)
        p = jnp.exp(s - m_new)
        l_scr[h] = alpha * l_scr[h] + jnp.sum(p, axis=-1, keepdims=True)
        acc_scr[h] = alpha * acc_scr[h] + jnp.dot(p.astype(BF16), vb, preferred_element_type=F32)
        m_scr[h] = m_new

    for h in range(DIFF_HEADS):
        q8 = q8_scr[h].astype(BF16)
        kb = _load_pages(k_refs, h).astype(BF16)
        vb = _load_pages(v_refs, h).astype(BF16)
        update(h, lax.dot_general(q8, kb, NT, preferred_element_type=F32) * scale, vb)

    @pl.when(step == N_PAGE_STEPS - 1)
    def _():
        lam = _diff_lambda(lq1_ref, lk1_ref, lq2_ref, lk2_ref, lam_init)
        for h in range(DIFF_HEADS):
            q8 = q8_scr[h].astype(BF16)
            _stage_new_kv(kn_ref, vn_ref, knp_scr, vnp_scr, h)
            sn = lax.dot_general(q8, knp_scr[...].astype(BF16), NT, preferred_element_type=F32) * scale
            update(h, jnp.where(_new_key_mask(), sn, NEG), vnp_scr[...].astype(BF16))
            o = acc_scr[h] / l_scr[h]
            o_ref[:, h * HEAD_DIM:(h + 1) * HEAD_DIM] = _diff_finish(
                o[0:DEC_SEQ, :], o[DEC_SEQ:2 * DEC_SEQ, :], lam, g_ref[...], lam_init)


def _tok_spec(c0):
    width = MOBA_HEADS * HEAD_DIM
    return pl.BlockSpec((None, DEC_SEQ, width), lambda b, *_: (b, 0, c0 // MOBA_HEADS))


def _diff_sample(proj3, cache_k, cache_v, page_table, layer, lams, subln_g, lam_init):
    lam_specs = [pl.BlockSpec((1, DIFF_QK_DIM), lambda b, s, pt: (0, 0))] * 4
    grid_spec = pltpu.PrefetchScalarGridSpec(
        num_scalar_prefetch=1,
        grid=(DEC_BATCH, N_PAGE_STEPS),
        in_specs=lam_specs + [pl.BlockSpec((1, HEAD_DIM), lambda b, s, pt: (0, 0)),
                              _tok_spec(COL_DQ), _tok_spec(COL_DK), _tok_spec(COL_DV)]
        + _page_specs(layer) + _page_specs(layer),
        out_specs=pl.BlockSpec((None, DEC_SEQ, DIFF_HEADS * HEAD_DIM), lambda b, s, pt: (b, 0, 0)),
        scratch_shapes=[pltpu.VMEM((DIFF_HEADS, SUBLANES, HEAD_DIM), F32),
                        pltpu.VMEM((PAGE_SIZE, HEAD_DIM), F32), pltpu.VMEM((PAGE_SIZE, HEAD_DIM), F32),
                        pltpu.VMEM((DIFF_HEADS, SUBLANES, 1), F32), pltpu.VMEM((DIFF_HEADS, SUBLANES, 1), F32),
                        pltpu.VMEM((DIFF_HEADS, SUBLANES, HEAD_DIM), F32)])
    return pl.pallas_call(
        functools.partial(_diff_sample_kernel, lam_init=lam_init),
        grid_spec=grid_spec,
        out_shape=jax.ShapeDtypeStruct((DEC_BATCH, DEC_SEQ, DIFF_HEADS * HEAD_DIM), F32),
        compiler_params=_cparams(("parallel", "arbitrary"), 56),
        name="diff_sample",
    )(page_table, *lams, subln_g.reshape(1, HEAD_DIM), proj3, proj3, proj3,
      *([cache_k] * PAGES_PER_STEP), *([cache_v] * PAGES_PER_STEP))


def _moba_sample_kernel(pt_ref, q_ref, kn_ref, vn_ref, *rest):
    k_refs = rest[:PAGES_PER_STEP]
    v_refs = rest[PAGES_PER_STEP:2 * PAGES_PER_STEP]
    o_ref = rest[2 * PAGES_PER_STEP]
    q8_scr, knp_scr, vnp_scr, s_scr, km_scr, p_scr, l_scr, acc_scr = rest[2 * PAGES_PER_STEP + 1:]
    phase = pl.program_id(1)
    step = pl.program_id(2)
    scale = HEAD_DIM ** -0.5

    @pl.when((phase == 0) & (step == 0))
    def _():
        q8_scr[...] = jnp.zeros(q8_scr.shape, F32)
        for h in range(MOBA_HEADS):
            q8_scr[h, 0:DEC_SEQ, :] = q_ref[:, h * HEAD_DIM:(h + 1) * HEAD_DIM]

    @pl.when(phase == 0)
    def _():
        for h in range(MOBA_HEADS):
            k = _load_pages(k_refs, h)
            km_scr[h, step] = (jnp.sum(k.reshape(BLOCKS_PER_STEP, MOBA_BLOCK, HEAD_DIM), axis=1)
                               * (1.0 / MOBA_BLOCK))
            s_scr[h, step] = lax.dot_general(q8_scr[h].astype(BF16), k.astype(BF16), NT,
                                             preferred_element_type=F32) * scale

    @pl.when((phase == 1) & (step == 0))
    def _():
        for h in range(MOBA_HEADS):
            q8 = q8_scr[h]
            km = km_scr[h].reshape(N_PAST_BLOCKS, HEAD_DIM)
            gate = lax.dot_general(q8, km, NT, precision=HI, preferred_element_type=F32)
            sel = _rank_lt(gate, N_PAST_BLOCKS, MOBA_TOPK, axis=1)
            _stage_new_kv(kn_ref, vn_ref, knp_scr, vnp_scr, h)
            sn = lax.dot_general(q8.astype(BF16), knp_scr[...].astype(BF16), NT,
                                 preferred_element_type=F32) * scale
            sn = jnp.where(_new_key_mask(), sn, NEG)
            m = jnp.max(sn, axis=-1, keepdims=True)

            def block(n, h=h, sel=sel):
                st, jb = divmod(n, BLOCKS_PER_STEP)
                return jnp.where(sel[:, n:n + 1] > 0,
                                 s_scr[h, st, :, jb * MOBA_BLOCK:(jb + 1) * MOBA_BLOCK], NEG)

            for n in range(N_PAST_BLOCKS):
                m = jnp.maximum(m, jnp.max(block(n), axis=-1, keepdims=True))
            pn = jnp.exp(sn - m)
            l = jnp.sum(pn, axis=-1, keepdims=True)
            for n in range(N_PAST_BLOCKS):
                st, jb = divmod(n, BLOCKS_PER_STEP)
                pb = jnp.exp(block(n) - m)
                l = l + jnp.sum(pb, axis=-1, keepdims=True)
                p_scr[h, st, :, jb * MOBA_BLOCK:(jb + 1) * MOBA_BLOCK] = pb
            l_scr[h] = l
            acc_scr[h] = jnp.dot(pn.astype(BF16), vnp_scr[...].astype(BF16), preferred_element_type=F32)

    @pl.when(phase == 1)
    def _():
        for h in range(MOBA_HEADS):
            vb = _load_pages(v_refs, h).astype(BF16)
            acc_scr[h] += jnp.dot(p_scr[h, step].astype(BF16), vb, preferred_element_type=F32)

    @pl.when((phase == 1) & (step == N_PAGE_STEPS - 1))
    def _():
        for h in range(MOBA_HEADS):
            o = acc_scr[h] / l_scr[h]
            o_ref[:, h * HEAD_DIM:(h + 1) * HEAD_DIM] = o[0:DEC_SEQ, :]


def _moba_sample(proj3, cache_k, cache_v, page_table, layer):
    grid_spec = pltpu.PrefetchScalarGridSpec(
        num_scalar_prefetch=1,
        grid=(DEC_BATCH, 2, N_PAGE_STEPS),
        in_specs=[_tok_spec(COL_MQ), _tok_spec(COL_MK), _tok_spec(COL_MV)]
        + _page_specs(layer, pinned_phase=1) + _page_specs(layer, pinned_phase=0),
        out_specs=pl.BlockSpec((None, DEC_SEQ, MOBA_HEADS * HEAD_DIM), lambda b, ph, s, pt: (b, 0, 0)),
        scratch_shapes=[pltpu.VMEM((MOBA_HEADS, SUBLANES, HEAD_DIM), F32),
                        pltpu.VMEM((PAGE_SIZE, HEAD_DIM), F32), pltpu.VMEM((PAGE_SIZE, HEAD_DIM), F32),
                        pltpu.VMEM((MOBA_HEADS, N_PAGE_STEPS, SUBLANES, KEYS_PER_STEP), F32),
                        pltpu.VMEM((MOBA_HEADS, N_PAGE_STEPS, BLOCKS_PER_STEP, HEAD_DIM), F32),
                        pltpu.VMEM((MOBA_HEADS, N_PAGE_STEPS, SUBLANES, KEYS_PER_STEP), F32),
                        pltpu.VMEM((MOBA_HEADS, SUBLANES, 1), F32),
                        pltpu.VMEM((MOBA_HEADS, SUBLANES, HEAD_DIM), F32)])
    return pl.pallas_call(
        _moba_sample_kernel,
        grid_spec=grid_spec,
        out_shape=jax.ShapeDtypeStruct((DEC_BATCH, DEC_SEQ, MOBA_HEADS * HEAD_DIM), F32),
        compiler_params=_cparams(("parallel", "arbitrary", "arbitrary"), 56),
        name="moba_sample",
    )(page_table, proj3, proj3, proj3, *([cache_k] * PAGES_PER_STEP), *([cache_v] * PAGES_PER_STEP))


def _out_proj_kernel(x_ref, c_ref, m_ref, d_ref, wc_ref, wm_ref, wd_ref, o_ref):
    acc = jnp.dot(c_ref[...].astype(BF16), wc_ref[...], preferred_element_type=F32)
    acc += jnp.dot(m_ref[...].astype(BF16), wm_ref[...], preferred_element_type=F32)
    acc += jnp.dot(d_ref[...].astype(BF16), wd_ref[...], preferred_element_type=F32)
    o_ref[...] = x_ref[...] + acc


def _out_proj(x, conv_o, moba_o, diff_o, wc, wm, wd, tm):
    t = x.shape[0]
    row = lambda w: pl.BlockSpec((tm, w), lambda i: (i, 0))
    full = lambda a: pl.BlockSpec(a.shape, lambda i: (0, 0))
    return pl.pallas_call(
        _out_proj_kernel,
        grid=(t // tm,),
        in_specs=[row(D_MODEL), row(conv_o.shape[1]), row(moba_o.shape[1]), row(diff_o.shape[1]),
                  full(wc), full(wm), full(wd)],
        out_specs=row(D_MODEL),
        out_shape=jax.ShapeDtypeStruct((t, D_MODEL), F32),
        compiler_params=_cparams(("parallel",), 48),
        name="out_proj",
    )(x, conv_o, moba_o, diff_o, wc, wm, wd)


def _top16_rows(s, row_id, id_bound):
    t = s.shape[1]
    slot = lax.broadcasted_iota(jnp.int32, (PEER_TOPK, t), 0)

    def body(k, carry):
        s, vals, idxs = carry
        m = jnp.max(s, axis=0, keepdims=True)
        idx = jnp.min(jnp.where(s == m, row_id, id_bound), axis=0, keepdims=True)
        vals = jnp.where(slot == k, m, vals)
        idxs = jnp.where(slot == k, idx, idxs)
        s = jnp.where(row_id == idx, -jnp.inf, s)
        return s, vals, idxs

    _, vals, idxs = lax.fori_loop(0, PEER_TOPK, body,
                                  (s, jnp.zeros((PEER_TOPK, t), F32), jnp.zeros((PEER_TOPK, t), jnp.int32)))
    return vals, idxs


def _route_kernel(q_ref, keys_ref, i1_ref, i2_ref, g_ref):
    t = q_ref.shape[0]
    key_id = lax.broadcasted_iota(jnp.int32, (PEER_N_KEYS, t), 0)
    r16 = lax.broadcasted_iota(jnp.int32, (PEER_TOPK, t), 0)
    r8 = lax.broadcasted_iota(jnp.int32, (SUBLANES, t), 0)
    pair_id = jnp.concatenate([r16] + [k1 * PEER_TOPK + r8 for k1 in range(1, SUBLANES)]
                              + [(r8 + SUBLANES) * PEER_TOPK], axis=0)
    i1_parts, i2_parts, g_parts = [], [], []
    for h in range(PEER_HEADS):
        tops = []
        for p in range(2):
            c0 = (h * 2 + p) * PEER_N_KEYS
            q_hp = q_ref[:, c0:c0 + PEER_N_KEYS]
            s_t = lax.dot_general(keys_ref[h, p], q_hp, NT, precision=HI, preferred_element_type=F32)
            tops.append(_top16_rows(s_t, key_id, PEER_N_KEYS))
        (v0, x0), (v1, x1) = tops
        cand = jnp.concatenate([v0[0:1, :] + v1]
                               + [v0[k1:k1 + 1, :] + v1[0:SUBLANES, :] for k1 in range(1, SUBLANES)]
                               + [v0[SUBLANES:PEER_TOPK, :] + v1[0:1, :]], axis=0)
        best, pos = _top16_rows(cand, pair_id, PEER_TOPK * PEER_TOPK)
        k1 = pos >> 4
        k2 = pos & (PEER_TOPK - 1)
        a = jnp.zeros(pos.shape, jnp.int32)
        b = jnp.zeros(pos.shape, jnp.int32)
        for k in range(PEER_TOPK):
            a = jnp.where(k1 == k, x0[k:k + 1, :], a)
            b = jnp.where(k2 == k, x1[k:k + 1, :], b)
        e = jnp.exp(best - jnp.max(best, axis=0, keepdims=True))
        g_parts.append(e / jnp.sum(e, axis=0, keepdims=True))
        i1_parts.append(a)
        i2_parts.append(b)
    i1_ref[...] = jnp.concatenate(i1_parts, axis=0).T
    i2_ref[...] = jnp.concatenate(i2_parts, axis=0).T
    g_ref[...] = jnp.concatenate(g_parts, axis=0).T


def _route(q, keys):
    t = q.shape[0]
    tb = min(4 * LANES, t)
    slots = PEER_HEADS * PEER_TOPK
    spec = pl.BlockSpec((tb, slots), lambda i: (i, 0))
    return pl.pallas_call(
        _route_kernel,
        grid=(t // tb,),
        in_specs=[pl.BlockSpec((tb, q.shape[1]), lambda i: (i, 0)),
                  pl.BlockSpec(keys.shape, lambda i: (0, 0, 0, 0))],
        out_specs=[spec, spec, spec],
        out_shape=[jax.ShapeDtypeStruct((t, slots), jnp.int32), jax.ShapeDtypeStruct((t, slots), jnp.int32),
                   jax.ShapeDtypeStruct((t, slots), F32)],
        compiler_params=_cparams(("parallel",), 32),
        name="peer_route",
    )(q, keys)


def _expert_kernel(x_ref, h_ref, i1_ref, i2_ref, g_ref, gf_ref, u_ref, v_ref, o_ref, w_scr, acc_scr, *,
                   final_norm):
    c = pl.program_id(1)
    tb = x_ref.shape[0]
    n_sub = PEER_EXPERT_CHUNK // PEER_N_KEYS
    w_stride = tb + W_PAD

    @pl.when(c == 0)
    def _():
        sub = lax.broadcasted_iota(jnp.int32, (PEER_N_KEYS, PEER_N_KEYS), 0)

        def build(t, carry):
            i1_row = i1_ref[pl.ds(t, 1), :]
            i2_row = i2_ref[pl.ds(t, 1), :]
            g_row = g_ref[pl.ds(t, 1), :]
            p_t = jnp.where(sub == i1_row, g_row, 0.0).astype(BF16)
            q_t = jnp.where(sub == i2_row, 1.0, 0.0).astype(BF16)
            w_t = lax.dot_general(p_t, q_t, NT, preferred_element_type=F32)
            w_scr[pl.ds(t, PEER_N_KEYS, stride=w_stride), :] = w_t
            return carry

        lax.fori_loop(0, tb, build, 0, unroll=16)
        acc_scr[...] = jnp.zeros(acc_scr.shape, F32)

    act = lax.dot_general(h_ref[...], u_ref[...], NT, preferred_element_type=F32)
    gel = 0.5 * act * (1.0 + lax.erf(act * math.sqrt(0.5)))
    parts = []
    for a in range(n_sub):
        w_a = w_scr[pl.ds(pl.multiple_of((c * n_sub + a) * w_stride, SUBLANES), tb), :]
        parts.append((w_a * gel[:, a * PEER_N_KEYS:(a + 1) * PEER_N_KEYS]).astype(BF16))
    wa = jnp.concatenate(parts, axis=1)
    acc_scr[...] += jnp.dot(wa, v_ref[...], preferred_element_type=F32)

    @pl.when(c == pl.num_programs(1) - 1)
    def _():
        y = x_ref[...] + acc_scr[...]
        if final_norm:
            ms = jnp.mean(y * y, axis=-1, keepdims=True)
            y = y * lax.rsqrt(ms + EPS) * gf_ref[...]
        o_ref[...] = y


def _experts(x, h, i1, i2, g, g_final, u, v, *, tb, final_norm):
    t = x.shape[0]
    slots = i1.shape[1]
    row = lambda w: pl.BlockSpec((tb, w), lambda i, c: (i, 0))
    chunk = pl.BlockSpec((PEER_EXPERT_CHUNK, D_MODEL), lambda i, c: (c, 0))
    return pl.pallas_call(
        functools.partial(_expert_kernel, final_norm=final_norm),
        grid=(t // tb, PEER_EXPERTS // PEER_EXPERT_CHUNK),
        in_specs=[row(D_MODEL), row(D_MODEL), row(slots), row(slots), row(slots),
                  pl.BlockSpec((1, D_MODEL), lambda i, c: (0, 0)), chunk, chunk],
        out_specs=row(D_MODEL),
        out_shape=jax.ShapeDtypeStruct((t, D_MODEL), F32),
        scratch_shapes=[pltpu.VMEM((PEER_N_KEYS * (tb + W_PAD), PEER_N_KEYS), F32),
                        pltpu.VMEM((tb, D_MODEL), F32)],
        compiler_params=_cparams(("parallel", "arbitrary"), 56),
        name="peer_experts",
    )(x, h, i1, i2, g, g_final.reshape(1, D_MODEL), u, v)


def _layer_tail(x, conv_o, moba_o, diff_o, lw, g_final, *, tm, tb, final_norm):
    x1 = _out_proj(x, conv_o, moba_o, diff_o, lw["wo_c"], lw["wo_m"], lw["wo_d"], tm=min(tm, 256))
    q, h2 = _project(x1, lw["g_ffn"], lw["wq"], emit_h=True, tm=tm, tn=1024)
    i1, i2, g = _route(q, lw["keys"])
    return _experts(x1, h2, i1, i2, g, g_final, lw["u"], lw["v"], tb=tb, final_norm=final_norm)


def kernel(x_prompt, x_sample, cache_moba_k, cache_moba_v, cache_diff_k, cache_diff_v, state_conv, page_table,
           g_mix, w_in, conv_w, diff_lambda_q1, diff_lambda_k1, diff_lambda_q2, diff_lambda_k2, diff_subln_g,
           w_out, g_ffn, peer_w_query, peer_sub_keys, peer_expert_u, peer_expert_v, g_final):
    xp = x_prompt.reshape(N_PROMPT, D_MODEL)
    xs = x_sample.reshape(N_SAMPLE, D_MODEL)
    cache_moba_k, cache_moba_v, cache_diff_k, cache_diff_v = (
        _head_major(c) for c in (cache_moba_k, cache_moba_v, cache_diff_k, cache_diff_v))
    prompt_state, sample_state = [], []
    for l in range(DEPTH):
        lam_init = 0.8 - 0.6 * math.exp(-0.3 * l)
        final = l == DEPTH - 1
        wo = w_out[l].astype(BF16)
        lw = dict(wo_c=wo[:CONV_CH], wo_m=wo[CONV_CH:CONV_CH + MOBA_HEADS * HEAD_DIM],
                  wo_d=wo[CONV_CH + MOBA_HEADS * HEAD_DIM:], g_ffn=g_ffn[l], wq=peer_w_query[l].astype(BF16),
                  keys=peer_sub_keys[l], u=peer_expert_u[l].astype(BF16), v=peer_expert_v[l].astype(BF16))
        w_in_b = w_in[l].astype(BF16)
        lams = [a[l].reshape(1, DIFF_QK_DIM) for a in
                (diff_lambda_q1, diff_lambda_k1, diff_lambda_q2, diff_lambda_k2)]

        proj, *new_kv = _project(xp, g_mix[l], w_in_b, seq=SEQ, tm=512, tn=STATE_WIDTH)
        conv_o, conv_new = _conv_prompt(proj, conv_w[l])
        moba_o = _moba_prompt(proj)
        diff_o = _diff_prompt(proj, lams, diff_subln_g[l], lam_init)
        xp = _layer_tail(xp, conv_o, moba_o, diff_o, lw, g_final, tm=512, tb=256, final_norm=final)
        prompt_state.append((new_kv, conv_new))

        proj_s = _project(xs, g_mix[l], w_in_b, tm=N_SAMPLE, tn=1024)
        proj3 = proj_s.reshape(DEC_BATCH, DEC_SEQ, proj_s.shape[1])
        conv_o, conv_new = _conv_sample(proj_s, conv_w[l], state_conv[l])
        moba_o = _moba_sample(proj3, cache_moba_k, cache_moba_v, page_table, l)
        diff_o = _diff_sample(proj3, cache_diff_k, cache_diff_v, page_table, l, lams, diff_subln_g[l], lam_init)
        xs = _layer_tail(xs, conv_o, moba_o.reshape(N_SAMPLE, -1), diff_o.reshape(N_SAMPLE, -1), lw, g_final,
                         tm=N_SAMPLE, tb=N_SAMPLE, final_norm=final)
        sample_state.append((proj_s, conv_new))

    def sample_states(group):
        cols = lambda c0: jnp.stack([p[:, c0 * LANES:(c0 + MOBA_HEADS) * LANES]
                                     .reshape(DEC_BATCH, DEC_SEQ, MOBA_HEADS, HEAD_DIM) for p, _ in group])
        return cols(COL_MK), cols(COL_MV), cols(COL_DK), cols(COL_DV), jnp.stack([c for _, c in group])

    def prompt_states(group):
        kv = tuple(jnp.transpose(jnp.stack([s[n] for s, _ in group]), (0, 1, 3, 2, 4))
                   for n in range(len(STATE_TILES)))
        return kv + (jnp.stack([c for _, c in group]),)

    y_prompt = xp.reshape(BATCH, SEQ, D_MODEL)
    y_sample = xs.reshape(DEC_BATCH, DEC_SEQ, D_MODEL)
    return (y_prompt, y_sample) + prompt_states(prompt_state) + sample_states(sample_state)
```

```python
import functools
import math

import jax
import jax.numpy as jnp
from jax import lax
from jax.experimental import pallas as pl
from jax.experimental.pallas import tpu as pltpu

D_MODEL = 2048
BATCH = 4
SEQ = 2048
DEPTH = 2
DEC_BATCH = 32
DEC_SEQ = 4
PAST_LEN = 8192
PAGE_SIZE = 128
HEAD_DIM = 128
CONV_CH = 512
CONV_WIDTH = 3
MOBA_HEADS = 6
MOBA_BLOCK = 256
MOBA_TOPK = 3
DIFF_HEADS = 6
DIFF_QK_DIM = 64
PEER_HEADS = 8
PEER_N_KEYS = 128
PEER_EXPERTS = PEER_N_KEYS * PEER_N_KEYS
PEER_TOPK = 16
EPS = 1e-6
NEG = -1e30

N_PAGES = PAST_LEN // PAGE_SIZE
N_SAMPLE = DEC_BATCH * DEC_SEQ
N_PROMPT = BATCH * SEQ
N_PROMPT_BLOCKS = SEQ // MOBA_BLOCK
N_PAST_BLOCKS = PAST_LEN // MOBA_BLOCK

COL_MQ, COL_MK, COL_MV = 12, 18, 24
COL_DQ, COL_DK, COL_DV = 30, 36, 42

LANES = 128
SUBLANES = 8
MIB = 1024 * 1024

PAGES_PER_STEP = 16
N_PAGE_STEPS = N_PAGES // PAGES_PER_STEP
KEYS_PER_STEP = PAGES_PER_STEP * PAGE_SIZE
BLOCKS_PER_STEP = KEYS_PER_STEP // MOBA_BLOCK

PEER_EXPERT_CHUNK = 1024
W_PAD = SUBLANES

F32 = jnp.float32
BF16 = jnp.bfloat16
HI = lax.Precision.HIGHEST
NT = (((1,), (1,)), ((), ()))


def _cparams(sem, vmem_mib):
    return pltpu.CompilerParams(dimension_semantics=sem, vmem_limit_bytes=vmem_mib * MIB)


STATE_WIDTH = MOBA_HEADS * HEAD_DIM
STATE_TILES = tuple(c // MOBA_HEADS for c in (COL_MK, COL_MV, COL_DK, COL_DV))


def _proj_kernel(x_ref, g_ref, w_ref, o_ref, *rest, emit_h, emit_states):
    rest = list(rest)
    hout_ref = rest.pop(0) if emit_h else None
    state_refs = [rest.pop(0) for _ in STATE_TILES] if emit_states else []
    h_scr = rest.pop(0)
    j = pl.program_id(1)

    @pl.when(j == 0)
    def _():
        x = x_ref[...]
        ms = jnp.mean(x * x, axis=-1, keepdims=True)
        hb = (x * lax.rsqrt(ms + EPS) * g_ref[...]).astype(BF16)
        h_scr[...] = hb
        if emit_h:
            hout_ref[...] = hb

    o_ref[...] = jnp.dot(h_scr[...], w_ref[...], preferred_element_type=F32)

    for tile, s_ref in zip(STATE_TILES, state_refs):
        @pl.when(j == tile)
        def _(s_ref=s_ref):
            for h in range(MOBA_HEADS):
                s_ref[h] = o_ref[:, h * HEAD_DIM:(h + 1) * HEAD_DIM]


def _project(x, g, w, *, emit_h=False, seq=None, tm, tn):
    t, d = x.shape
    n = w.shape[1]
    emit_states = seq is not None
    out_shape = [jax.ShapeDtypeStruct((t, n), F32)]
    out_specs = [pl.BlockSpec((tm, tn), lambda i, j: (i, j))]
    if emit_h:
        out_shape.append(jax.ShapeDtypeStruct((t, d), BF16))
        out_specs.append(pl.BlockSpec((tm, d), lambda i, j: (i, 0)))
    if emit_states:
        assert tn == STATE_WIDTH and seq % tm == 0
        per_seq = seq // tm
        for _ in STATE_TILES:
            out_shape.append(jax.ShapeDtypeStruct((t // seq, MOBA_HEADS, seq, HEAD_DIM), F32))
            out_specs.append(pl.BlockSpec((None, MOBA_HEADS, tm, HEAD_DIM),
                                          lambda i, j: (i // per_seq, 0, i % per_seq, 0)))
    res = pl.pallas_call(
        functools.partial(_proj_kernel, emit_h=emit_h, emit_states=emit_states),
        grid=(t // tm, n // tn),
        in_specs=[pl.BlockSpec((tm, d), lambda i, j: (i, 0)),
                  pl.BlockSpec((1, d), lambda i, j: (0, 0)),
                  pl.BlockSpec((d, tn), lambda i, j: (0, j))],
        out_specs=out_specs, out_shape=out_shape,
        scratch_shapes=[pltpu.VMEM((tm, d), BF16)],
        compiler_params=_cparams(("parallel", "arbitrary"), 48),
        name="proj",
    )(x, g.reshape(1, d), w)
    return res if len(res) > 1 else res[0]


def _conv_prompt_kernel(cb_ref, cc_ref, cx_ref, w_ref, y_ref, st_ref, u_scr):
    t = cb_ref.shape[0]
    u = cc_ref[...] * cx_ref[...]
    u_scr[0:SUBLANES, :] = jnp.zeros((SUBLANES, CONV_CH), F32)
    u_scr[SUBLANES:SUBLANES + t, :] = u
    w = w_ref[...]
    y = w[0:1, :] * u_scr[SUBLANES - 2:SUBLANES - 2 + t, :]
    y = y + w[1:2, :] * u_scr[SUBLANES - 1:SUBLANES - 1 + t, :]
    y = y + w[2:3, :] * u
    y_ref[...] = cb_ref[...] * y
    st_ref[...] = u_scr[SUBLANES + t - 2:SUBLANES + t, :]


def _conv_prompt(proj, conv_w):
    return pl.pallas_call(
        _conv_prompt_kernel,
        grid=(BATCH,),
        in_specs=[pl.BlockSpec((SEQ, CONV_CH), lambda b: (b, 0)),
                  pl.BlockSpec((SEQ, CONV_CH), lambda b: (b, 1)),
                  pl.BlockSpec((SEQ, CONV_CH), lambda b: (b, 2)),
                  pl.BlockSpec((CONV_WIDTH, CONV_CH), lambda b: (0, 0))],
        out_specs=[pl.BlockSpec((SEQ, CONV_CH), lambda b: (b, 0)),
                   pl.BlockSpec((None, CONV_WIDTH - 1, CONV_CH), lambda b: (b, 0, 0))],
        out_shape=[jax.ShapeDtypeStruct((N_PROMPT, CONV_CH), F32),
                   jax.ShapeDtypeStruct((BATCH, CONV_WIDTH - 1, CONV_CH), F32)],
        scratch_shapes=[pltpu.VMEM((SEQ + SUBLANES, CONV_CH), F32)],
        compiler_params=_cparams(("parallel",), 48),
        name="conv_prompt",
    )(proj, proj, proj, conv_w)


def _conv_sample_kernel(cb_ref, cc_ref, cx_ref, w_ref, st_in_ref, y_ref, st_ref, u_scr):
    w = w_ref[...]
    for s in range(DEC_BATCH):
        r = s * DEC_SEQ
        u_scr[0:2, :] = st_in_ref[s]
        u_scr[2:2 + DEC_SEQ, :] = cc_ref[r:r + DEC_SEQ, :] * cx_ref[r:r + DEC_SEQ, :]
        y = w[0:1, :] * u_scr[0:DEC_SEQ, :]
        y = y + w[1:2, :] * u_scr[1:1 + DEC_SEQ, :]
        y = y + w[2:3, :] * u_scr[2:2 + DEC_SEQ, :]
        y_ref[r:r + DEC_SEQ, :] = cb_ref[r:r + DEC_SEQ, :] * y
        st_ref[s] = u_scr[DEC_SEQ:DEC_SEQ + 2, :]


def _conv_sample(proj, conv_w, state):
    return pl.pallas_call(
        _conv_sample_kernel,
        grid=(1,),
        in_specs=[pl.BlockSpec((N_SAMPLE, CONV_CH), lambda i: (0, 0)),
                  pl.BlockSpec((N_SAMPLE, CONV_CH), lambda i: (0, 1)),
                  pl.BlockSpec((N_SAMPLE, CONV_CH), lambda i: (0, 2)),
                  pl.BlockSpec((CONV_WIDTH, CONV_CH), lambda i: (0, 0)),
                  pl.BlockSpec((DEC_BATCH, CONV_WIDTH - 1, CONV_CH), lambda i: (0, 0, 0))],
        out_specs=[pl.BlockSpec((N_SAMPLE, CONV_CH), lambda i: (0, 0)),
                   pl.BlockSpec((DEC_BATCH, CONV_WIDTH - 1, CONV_CH), lambda i: (0, 0, 0))],
        out_shape=[jax.ShapeDtypeStruct((N_SAMPLE, CONV_CH), F32),
                   jax.ShapeDtypeStruct((DEC_BATCH, CONV_WIDTH - 1, CONV_CH), F32)],
        scratch_shapes=[pltpu.VMEM((SUBLANES, CONV_CH), F32)],
        name="conv_sample",
    )(proj, proj, proj, conv_w, state)


def _rank_lt(gate, n_blocks, k, axis):
    block_id = lax.broadcasted_iota(jnp.int32, gate.shape, axis)
    rank = jnp.zeros(gate.shape, jnp.int32)
    for n2 in range(n_blocks):
        other = gate[n2:n2 + 1, :] if axis == 0 else gate[:, n2:n2 + 1]
        beats = jnp.where(other > gate, 1, jnp.where(other == gate, jnp.where(block_id > n2, 1, 0), 0))
        rank = rank + beats
    return jnp.where(rank < k, 1, 0)


def _for_past_block_pairs(i, update):
    for a in range(0, N_PROMPT_BLOCKS - 1, 2):
        b = a + 1
        if b < N_PROMPT_BLOCKS - 1:
            pl.when(b < i)(functools.partial(update, (a, b)))
        pl.when(b == i)(functools.partial(update, (a,)))


HEADS_PER_STEP = 2


def _head_cols(hh):
    return slice(hh * HEAD_DIM, (hh + 1) * HEAD_DIM)


def _stage_kv(k_ref, v_ref, kb_scr, vt_scr):
    for hh in range(HEADS_PER_STEP):
        kb_scr[hh] = k_ref[:, _head_cols(hh)].astype(BF16)
        for n in range(N_PROMPT_BLOCKS):
            vt_scr[hh, n] = v_ref[n * MOBA_BLOCK:(n + 1) * MOBA_BLOCK, _head_cols(hh)].T.astype(BF16)


def _prompt_specs(col_q, col_k, col_v):
    tq, width = MOBA_BLOCK, HEADS_PER_STEP * HEAD_DIM
    q_spec = pl.BlockSpec((tq, width), lambda b, g, i: (b * N_PROMPT_BLOCKS + i, col_q // HEADS_PER_STEP + g))
    k_spec = pl.BlockSpec((SEQ, width), lambda b, g, i: (b, col_k // HEADS_PER_STEP + g))
    v_spec = pl.BlockSpec((SEQ, width), lambda b, g, i: (b, col_v // HEADS_PER_STEP + g))
    o_spec = pl.BlockSpec((tq, width), lambda b, g, i: (b * N_PROMPT_BLOCKS + i, g))
    return q_spec, k_spec, v_spec, o_spec


def _moba_prompt_kernel(q_ref, k_ref, v_ref, o_ref, kb_scr, vt_scr, km_scr, m_scr, l_scr, acc_scr):
    i = pl.program_id(2)
    tq = MOBA_BLOCK
    scale = HEAD_DIM ** -0.5

    @pl.when(i == 0)
    def _():
        _stage_kv(k_ref, v_ref, kb_scr, vt_scr)
        for hh in range(HEADS_PER_STEP):
            k = k_ref[:, _head_cols(hh)]
            km_scr[hh] = jnp.sum(k.reshape(N_PROMPT_BLOCKS, MOBA_BLOCK, HEAD_DIM), axis=1) * (1.0 / MOBA_BLOCK)

    own = pl.multiple_of(i * tq, tq)
    key = lax.broadcasted_iota(jnp.int32, (tq, tq), 0)
    qry = lax.broadcasted_iota(jnp.int32, (tq, tq), 1)
    sels, qbs = [], []
    for hh in range(HEADS_PER_STEP):
        q = q_ref[:, _head_cols(hh)]
        gate = lax.dot_general(km_scr[hh], q, NT, precision=HI, preferred_element_type=F32)
        blk = lax.broadcasted_iota(jnp.int32, gate.shape, 0)
        gate = jnp.where(blk < i, gate, NEG)
        sels.append(_rank_lt(gate, N_PROMPT_BLOCKS, MOBA_TOPK, axis=0) * jnp.where(blk < i, 1, 0))
        qb = q.astype(BF16)
        qbs.append(qb)
        s = lax.dot_general(kb_scr[hh, pl.ds(own, tq), :], qb, NT, preferred_element_type=F32) * scale
        s = jnp.where(key <= qry, s, NEG)
        m = jnp.max(s, axis=0, keepdims=True)
        p = jnp.exp(s - m)
        m_scr[hh] = m
        l_scr[hh] = jnp.sum(p, axis=0, keepdims=True)
        acc_scr[hh] = jnp.dot(vt_scr[hh, i], p.astype(BF16), preferred_element_type=F32)

    def past_update(blocks):
        for hh in range(HEADS_PER_STEP):
            scores = []
            for n in blocks:
                sn = lax.dot_general(kb_scr[hh, n * tq:(n + 1) * tq, :], qbs[hh], NT,
                                     preferred_element_type=F32) * scale
                scores.append(jnp.where(sels[hh][n:n + 1, :] > 0, sn, NEG))
            m_old = m_scr[hh]
            m_new = m_old
            for sn in scores:
                m_new = jnp.maximum(m_new, jnp.max(sn, axis=0, keepdims=True))
            alpha = jnp.exp(m_old - m_new)
            l = alpha * l_scr[hh]
            acc = alpha * acc_scr[hh]
            for n, sn in zip(blocks, scores):
                pn = jnp.exp(sn - m_new)
                l = l + jnp.sum(pn, axis=0, keepdims=True)
                acc = acc + jnp.dot(vt_scr[hh, n], pn.astype(BF16), preferred_element_type=F32)
            l_scr[hh] = l
            acc_scr[hh] = acc
            m_scr[hh] = m_new

    _for_past_block_pairs(i, past_update)
    for hh in range(HEADS_PER_STEP):
        o_ref[:, _head_cols(hh)] = (acc_scr[hh] / l_scr[hh]).T


def _moba_prompt(proj):
    tq, hps = MOBA_BLOCK, HEADS_PER_STEP
    q_spec, k_spec, v_spec, o_spec = _prompt_specs(COL_MQ, COL_MK, COL_MV)
    return pl.pallas_call(
        _moba_prompt_kernel,
        grid=(BATCH, MOBA_HEADS // hps, N_PROMPT_BLOCKS),
        in_specs=[q_spec, k_spec, v_spec],
        out_specs=o_spec,
        out_shape=jax.ShapeDtypeStruct((N_PROMPT, MOBA_HEADS * HEAD_DIM), F32),
        scratch_shapes=[pltpu.VMEM((hps, SEQ, HEAD_DIM), BF16),
                        pltpu.VMEM((hps, N_PROMPT_BLOCKS, HEAD_DIM, tq), BF16),
                        pltpu.VMEM((hps, N_PROMPT_BLOCKS, HEAD_DIM), F32),
                        pltpu.VMEM((hps, 1, tq), F32), pltpu.VMEM((hps, 1, tq), F32),
                        pltpu.VMEM((hps, HEAD_DIM, tq), F32)],
        compiler_params=_cparams(("parallel", "parallel", "arbitrary"), 32),
        name="moba_prompt",
    )(proj, proj, proj)


def _diff_lambda(lq1_ref, lk1_ref, lq2_ref, lk2_ref, lam_init):
    a = jnp.sum(lq1_ref[...] * lk1_ref[...], axis=-1, keepdims=True)
    b = jnp.sum(lq2_ref[...] * lk2_ref[...], axis=-1, keepdims=True)
    return jnp.exp(a) - jnp.exp(b) + lam_init


def _diff_finish(o1, o2, lam, g, lam_init, axis=-1):
    o = o1 - lam * o2
    ms = jnp.mean(o * o, axis=axis, keepdims=True)
    return o * lax.rsqrt(ms + EPS) * g * (1.0 - lam_init)


def _split_maps(q):
    lane = lax.broadcasted_iota(jnp.int32, q.shape, 1)
    return jnp.where(lane < DIFF_QK_DIM, q, 0.0), jnp.where(lane >= DIFF_QK_DIM, q, 0.0)


def _diff_prompt_kernel(lq1_ref, lk1_ref, lq2_ref, lk2_ref, g_ref, q_ref, k_ref, v_ref, o_ref,
                        kb_scr, vt_scr, m_scr, l_scr, acc_scr, *, lam_init):
    i = pl.program_id(2)
    tq = MOBA_BLOCK
    scale = DIFF_QK_DIM ** -0.5

    @pl.when(i == 0)
    def _():
        _stage_kv(k_ref, v_ref, kb_scr, vt_scr)

    own = pl.multiple_of(i * tq, tq)
    key = lax.broadcasted_iota(jnp.int32, (tq, tq), 0)
    qry = lax.broadcasted_iota(jnp.int32, (tq, tq), 1)
    chains = []
    for hh in range(HEADS_PER_STEP):
        for c, qc in enumerate(_split_maps(q_ref[:, _head_cols(hh)])):
            chains.append((hh, 2 * hh + c, qc.astype(BF16)))
    for hh, slot, qc in chains:
        s = lax.dot_general(kb_scr[hh, pl.ds(own, tq), :], qc, NT, preferred_element_type=F32) * scale
        s = jnp.where(key <= qry, s, NEG)
        m = jnp.max(s, axis=0, keepdims=True)
        p = jnp.exp(s - m)
        m_scr[slot] = m
        l_scr[slot] = jnp.sum(p, axis=0, keepdims=True)
        acc_scr[slot] = jnp.dot(vt_scr[hh, i], p.astype(BF16), preferred_element_type=F32)

    def past_update(blocks):
        for hh, slot, qc in chains:
            scores = [lax.dot_general(kb_scr[hh, n * tq:(n + 1) * tq, :], qc, NT,
                                      preferred_element_type=F32) * scale for n in blocks]
            m_old = m_scr[slot]
            m_new = m_old
            for sn in scores:
                m_new = jnp.maximum(m_new, jnp.max(sn, axis=0, keepdims=True))
            alpha = jnp.exp(m_old - m_new)
            l = alpha * l_scr[slot]
            acc = alpha * acc_scr[slot]
            for n, sn in zip(blocks, scores):
                pn = jnp.exp(sn - m_new)
                l = l + jnp.sum(pn, axis=0, keepdims=True)
                acc = acc + jnp.dot(vt_scr[hh, n], pn.astype(BF16), preferred_element_type=F32)
            l_scr[slot] = l
            acc_scr[slot] = acc
            m_scr[slot] = m_new

    _for_past_block_pairs(i, past_update)
    lam = _diff_lambda(lq1_ref, lk1_ref, lq2_ref, lk2_ref, lam_init)
    for hh in range(HEADS_PER_STEP):
        y = _diff_finish(acc_scr[2 * hh] / l_scr[2 * hh], acc_scr[2 * hh + 1] / l_scr[2 * hh + 1], lam,
                         g_ref[...], lam_init, axis=0)
        o_ref[:, _head_cols(hh)] = y.T


def _diff_prompt(proj, lams, subln_g, lam_init):
    tq, hps = MOBA_BLOCK, HEADS_PER_STEP
    lam_specs = [pl.BlockSpec((1, DIFF_QK_DIM), lambda b, h, i: (0, 0))] * 4
    q_spec, k_spec, v_spec, o_spec = _prompt_specs(COL_DQ, COL_DK, COL_DV)
    return pl.pallas_call(
        functools.partial(_diff_prompt_kernel, lam_init=lam_init),
        grid=(BATCH, DIFF_HEADS // hps, N_PROMPT_BLOCKS),
        in_specs=lam_specs + [pl.BlockSpec((HEAD_DIM, 1), lambda b, h, i: (0, 0)), q_spec, k_spec, v_spec],
        out_specs=o_spec,
        out_shape=jax.ShapeDtypeStruct((N_PROMPT, DIFF_HEADS * HEAD_DIM), F32),
        scratch_shapes=[pltpu.VMEM((hps, SEQ, HEAD_DIM), BF16),
                        pltpu.VMEM((hps, N_PROMPT_BLOCKS, HEAD_DIM, tq), BF16),
                        pltpu.VMEM((2 * hps, 1, tq), F32), pltpu.VMEM((2 * hps, 1, tq), F32),
                        pltpu.VMEM((2 * hps, HEAD_DIM, tq), F32)],
        compiler_params=_cparams(("parallel", "parallel", "arbitrary"), 32),
        name="diff_prompt",
    )(*lams, subln_g.reshape(HEAD_DIM, 1), proj, proj, proj)


def _page_specs(layer, pinned_phase=None):
    specs = []
    for r in range(PAGES_PER_STEP):
        if pinned_phase is None:
            def imap(b, s, pt, r=r):
                return (layer, pt[b, s * PAGES_PER_STEP + r], 0, 0, 0)
        elif pinned_phase == 1:
            def imap(b, ph, s, pt, r=r):
                st = jnp.where(ph == 0, s, N_PAGE_STEPS - 1)
                return (layer, pt[b, st * PAGES_PER_STEP + r], 0, 0, 0)
        else:
            def imap(b, ph, s, pt, r=r):
                st = jnp.where(ph == 0, 0, s)
                return (layer, pt[b, st * PAGES_PER_STEP + r], 0, 0, 0)
        specs.append(pl.BlockSpec((None, None, MOBA_HEADS, PAGE_SIZE, HEAD_DIM), imap))
    return specs


def _head_major(cache):
    return jnp.transpose(cache, (0, 1, 3, 2, 4))


def _load_pages(refs, h):
    return jnp.concatenate([r[h] for r in refs], axis=0)


def _new_key_mask():
    r = lax.broadcasted_iota(jnp.int32, (SUBLANES, PAGE_SIZE), 0)
    j = lax.broadcasted_iota(jnp.int32, (SUBLANES, PAGE_SIZE), 1)
    return (j <= (r & (DEC_SEQ - 1))) & (j < DEC_SEQ)


def _stage_new_kv(kn_ref, vn_ref, knp_scr, vnp_scr, h):
    knp_scr[...] = jnp.zeros(knp_scr.shape, F32)
    vnp_scr[...] = jnp.zeros(vnp_scr.shape, F32)
    knp_scr[0:DEC_SEQ, :] = kn_ref[:, h * HEAD_DIM:(h + 1) * HEAD_DIM]
    vnp_scr[0:DEC_SEQ, :] = vn_ref[:, h * HEAD_DIM:(h + 1) * HEAD_DIM]


def _diff_sample_kernel(pt_ref, lq1_ref, lk1_ref, lq2_ref, lk2_ref, g_ref, q_ref, kn_ref, vn_ref, *rest,
                        lam_init):
    k_refs = rest[:PAGES_PER_STEP]
    v_refs = rest[PAGES_PER_STEP:2 * PAGES_PER_STEP]
    o_ref = rest[2 * PAGES_PER_STEP]
    q8_scr, knp_scr, vnp_scr, m_scr, l_scr, acc_scr = rest[2 * PAGES_PER_STEP + 1:]
    step = pl.program_id(1)
    scale = DIFF_QK_DIM ** -0.5

    @pl.when(step == 0)
    def _():
        for h in range(DIFF_HEADS):
            q1, q2 = _split_maps(q_ref[:, h * HEAD_DIM:(h + 1) * HEAD_DIM])
            q8_scr[h, 0:DEC_SEQ, :] = q1
            q8_scr[h, DEC_SEQ:2 * DEC_SEQ, :] = q2
        m_scr[...] = jnp.full(m_scr.shape, -jnp.inf, F32)
        l_scr[...] = jnp.zeros(l_scr.shape, F32)
        acc_scr[...] = jnp.zeros(acc_scr.shape, F32)

    def update(h, s, vb):
        m_old = m_scr[h]
        m_new = jnp.maximum(m_old, jnp.max(s, axis=-1, keepdims=True))
        alpha = jnp.exp(m_old - m_new)
        p = jnp.exp(s - m_new)
        l_scr[h] = alpha * l_scr[h] + jnp.sum(p, axis=-1, keepdims=True)
        acc_scr[h] = alpha * acc_scr[h] + jnp.dot(p.astype(BF16), vb, preferred_element_type=F32)
        m_scr[h] = m_new

    for h in range(DIFF_HEADS):
        q8 = q8_scr[h].astype(BF16)
        kb = _load_pages(k_refs, h).astype(BF16)
        vb = _load_pages(v_refs, h).astype(BF16)
        update(h, lax.dot_general(q8, kb, NT, preferred_element_type=F32) * scale, vb)

    @pl.when(step == N_PAGE_STEPS - 1)
    def _():
        lam = _diff_lambda(lq1_ref, lk1_ref, lq2_ref, lk2_ref, lam_init)
        for h in range(DIFF_HEADS):
            q8 = q8_scr[h].astype(BF16)
            _stage_new_kv(kn_ref, vn_ref, knp_scr, vnp_scr, h)
            sn = lax.dot_general(q8, knp_scr[...].astype(BF16), NT, preferred_element_type=F32) * scale
            update(h, jnp.where(_new_key_mask(), sn, NEG), vnp_scr[...].astype(BF16))
            o = acc_scr[h] / l_scr[h]
            o_ref[:, h * HEAD_DIM:(h + 1) * HEAD_DIM] = _diff_finish(
                o[0:DEC_SEQ, :], o[DEC_SEQ:2 * DEC_SEQ, :], lam, g_ref[...], lam_init)


def _tok_spec(c0):
    width = MOBA_HEADS * HEAD_DIM
    return pl.BlockSpec((None, DEC_SEQ, width), lambda b, *_: (b, 0, c0 // MOBA_HEADS))


def _diff_sample(proj3, cache_k, cache_v, page_table, layer, lams, subln_g, lam_init):
    lam_specs = [pl.BlockSpec((1, DIFF_QK_DIM), lambda b, s, pt: (0, 0))] * 4
    grid_spec = pltpu.PrefetchScalarGridSpec(
        num_scalar_prefetch=1,
        grid=(DEC_BATCH, N_PAGE_STEPS),
        in_specs=lam_specs + [pl.BlockSpec((1, HEAD_DIM), lambda b, s, pt: (0, 0)),
                              _tok_spec(COL_DQ), _tok_spec(COL_DK), _tok_spec(COL_DV)]
        + _page_specs(layer) + _page_specs(layer),
        out_specs=pl.BlockSpec((None, DEC_SEQ, DIFF_HEADS * HEAD_DIM), lambda b, s, pt: (b, 0, 0)),
        scratch_shapes=[pltpu.VMEM((DIFF_HEADS, SUBLANES, HEAD_DIM), F32),
                        pltpu.VMEM((PAGE_SIZE, HEAD_DIM), F32), pltpu.VMEM((PAGE_SIZE, HEAD_DIM), F32),
                        pltpu.VMEM((DIFF_HEADS, SUBLANES, 1), F32), pltpu.VMEM((DIFF_HEADS, SUBLANES, 1), F32),
                        pltpu.VMEM((DIFF_HEADS, SUBLANES, HEAD_DIM), F32)])
    return pl.pallas_call(
        functools.partial(_diff_sample_kernel, lam_init=lam_init),
        grid_spec=grid_spec,
        out_shape=jax.ShapeDtypeStruct((DEC_BATCH, DEC_SEQ, DIFF_HEADS * HEAD_DIM), F32),
        compiler_params=_cparams(("parallel", "arbitrary"), 56),
        name="diff_sample",
    )(page_table, *lams, subln_g.reshape(1, HEAD_DIM), proj3, proj3, proj3,
      *([cache_k] * PAGES_PER_STEP), *([cache_v] * PAGES_PER_STEP))


def _moba_sample_kernel(pt_ref, q_ref, kn_ref, vn_ref, *rest):
    k_refs = rest[:PAGES_PER_STEP]
    v_refs = rest[PAGES_PER_STEP:2 * PAGES_PER_STEP]
    o_ref = rest[2 * PAGES_PER_STEP]
    q8_scr, knp_scr, vnp_scr, s_scr, km_scr, p_scr, l_scr, acc_scr = rest[2 * PAGES_PER_STEP + 1:]
    phase = pl.program_id(1)
    step = pl.program_id(2)
    scale = HEAD_DIM ** -0.5

    @pl.when((phase == 0) & (step == 0))
    def _():
        q8_scr[...] = jnp.zeros(q8_scr.shape, F32)
        for h in range(MOBA_HEADS):
            q8_scr[h, 0:DEC_SEQ, :] = q_ref[:, h * HEAD_DIM:(h + 1) * HEAD_DIM]

    @pl.when(phase == 0)
    def _():
        for h in range(MOBA_HEADS):
            k = _load_pages(k_refs, h)
            km_scr[h, step] = (jnp.sum(k.reshape(BLOCKS_PER_STEP, MOBA_BLOCK, HEAD_DIM), axis=1)
                               * (1.0 / MOBA_BLOCK))
            s_scr[h, step] = lax.dot_general(q8_scr[h].astype(BF16), k.astype(BF16), NT,
                                             preferred_element_type=F32) * scale

    @pl.when((phase == 1) & (step == 0))
    def _():
        for h in range(MOBA_HEADS):
            q8 = q8_scr[h]
            km = km_scr[h].reshape(N_PAST_BLOCKS, HEAD_DIM)
            gate = lax.dot_general(q8, km, NT, precision=HI, preferred_element_type=F32)
            sel = _rank_lt(gate, N_PAST_BLOCKS, MOBA_TOPK, axis=1)
            _stage_new_kv(kn_ref, vn_ref, knp_scr, vnp_scr, h)
            sn = lax.dot_general(q8.astype(BF16), knp_scr[...].astype(BF16), NT,
                                 preferred_element_type=F32) * scale
            sn = jnp.where(_new_key_mask(), sn, NEG)
            m = jnp.max(sn, axis=-1, keepdims=True)

            def block(n, h=h, sel=sel):
                st, jb = divmod(n, BLOCKS_PER_STEP)
                return jnp.where(sel[:, n:n + 1] > 0,
                                 s_scr[h, st, :, jb * MOBA_BLOCK:(jb + 1) * MOBA_BLOCK], NEG)

            for n in range(N_PAST_BLOCKS):
                m = jnp.maximum(m, jnp.max(block(n), axis=-1, keepdims=True))
            pn = jnp.exp(sn - m)
            l = jnp.sum(pn, axis=-1, keepdims=True)
            for n in range(N_PAST_BLOCKS):
                st, jb = divmod(n, BLOCKS_PER_STEP)
                pb = jnp.exp(block(n) - m)
                l = l + jnp.sum(pb, axis=-1, keepdims=True)
                p_scr[h, st, :, jb * MOBA_BLOCK:(jb + 1) * MOBA_BLOCK] = pb
            l_scr[h] = l
            acc_scr[h] = jnp.dot(pn.astype(BF16), vnp_scr[...].astype(BF16), preferred_element_type=F32)

    @pl.when(phase == 1)
    def _():
        for h in range(MOBA_HEADS):
            vb = _load_pages(v_refs, h).astype(BF16)
            acc_scr[h] += jnp.dot(p_scr[h, step].astype(BF16), vb, preferred_element_type=F32)

    @pl.when((phase == 1) & (step == N_PAGE_STEPS - 1))
    def _():
        for h in range(MOBA_HEADS):
            o = acc_scr[h] / l_scr[h]
            o_ref[:, h * HEAD_DIM:(h + 1) * HEAD_DIM] = o[0:DEC_SEQ, :]


def _moba_sample(proj3, cache_k, cache_v, page_table, layer):
    grid_spec = pltpu.PrefetchScalarGridSpec(
        num_scalar_prefetch=1,
        grid=(DEC_BATCH, 2, N_PAGE_STEPS),
        in_specs=[_tok_spec(COL_MQ), _tok_spec(COL_MK), _tok_spec(COL_MV)]
        + _page_specs(layer, pinned_phase=1) + _page_specs(layer, pinned_phase=0),
        out_specs=pl.BlockSpec((None, DEC_SEQ, MOBA_HEADS * HEAD_DIM), lambda b, ph, s, pt: (b, 0, 0)),
        scratch_shapes=[pltpu.VMEM((MOBA_HEADS, SUBLANES, HEAD_DIM), F32),
                        pltpu.VMEM((PAGE_SIZE, HEAD_DIM), F32), pltpu.VMEM((PAGE_SIZE, HEAD_DIM), F32),
                        pltpu.VMEM((MOBA_HEADS, N_PAGE_STEPS, SUBLANES, KEYS_PER_STEP), F32),
                        pltpu.VMEM((MOBA_HEADS, N_PAGE_STEPS, BLOCKS_PER_STEP, HEAD_DIM), F32),
                        pltpu.VMEM((MOBA_HEADS, N_PAGE_STEPS, SUBLANES, KEYS_PER_STEP), F32),
                        pltpu.VMEM((MOBA_HEADS, SUBLANES, 1), F32),
                        pltpu.VMEM((MOBA_HEADS, SUBLANES, HEAD_DIM), F32)])
    return pl.pallas_call(
        _moba_sample_kernel,
        grid_spec=grid_spec,
        out_shape=jax.ShapeDtypeStruct((DEC_BATCH, DEC_SEQ, MOBA_HEADS * HEAD_DIM), F32),
        compiler_params=_cparams(("parallel", "arbitrary", "arbitrary"), 56),
        name="moba_sample",
    )(page_table, proj3, proj3, proj3, *([cache_k] * PAGES_PER_STEP), *([cache_v] * PAGES_PER_STEP))


def _out_proj_kernel(x_ref, c_ref, m_ref, d_ref, wc_ref, wm_ref, wd_ref, o_ref):
    acc = jnp.dot(c_ref[...].astype(BF16), wc_ref[...], preferred_element_type=F32)
    acc += jnp.dot(m_ref[...].astype(BF16), wm_ref[...], preferred_element_type=F32)
    acc += jnp.dot(d_ref[...].astype(BF16), wd_ref[...], preferred_element_type=F32)
    o_ref[...] = x_ref[...] + acc


def _out_proj(x, conv_o, moba_o, diff_o, wc, wm, wd, tm):
    t = x.shape[0]
    row = lambda w: pl.BlockSpec((tm, w), lambda i: (i, 0))
    full = lambda a: pl.BlockSpec(a.shape, lambda i: (0, 0))
    return pl.pallas_call(
        _out_proj_kernel,
        grid=(t // tm,),
        in_specs=[row(D_MODEL), row(conv_o.shape[1]), row(moba_o.shape[1]), row(diff_o.shape[1]),
                  full(wc), full(wm), full(wd)],
        out_specs=row(D_MODEL),
        out_shape=jax.ShapeDtypeStruct((t, D_MODEL), F32),
        compiler_params=_cparams(("parallel",), 48),
        name="out_proj",
    )(x, conv_o, moba_o, diff_o, wc, wm, wd)


def _top16_rows(s, row_id, id_bound):
    t = s.shape[1]
    slot = lax.broadcasted_iota(jnp.int32, (PEER_TOPK, t), 0)

    def body(k, carry):
        s, vals, idxs = carry
        m = jnp.max(s, axis=0, keepdims=True)
        idx = jnp.min(jnp.where(s == m, row_id, id_bound), axis=0, keepdims=True)
        vals = jnp.where(slot == k, m, vals)
        idxs = jnp.where(slot == k, idx, idxs)
        s = jnp.where(row_id == idx, -jnp.inf, s)
        return s, vals, idxs

    _, vals, idxs = lax.fori_loop(0, PEER_TOPK, body,
                                  (s, jnp.zeros((PEER_TOPK, t), F32), jnp.zeros((PEER_TOPK, t), jnp.int32)))
    return vals, idxs


def _route_kernel(q_ref, keys_ref, i1_ref, i2_ref, g_ref):
    t = q_ref.shape[0]
    key_id = lax.broadcasted_iota(jnp.int32, (PEER_N_KEYS, t), 0)
    r16 = lax.broadcasted_iota(jnp.int32, (PEER_TOPK, t), 0)
    r8 = lax.broadcasted_iota(jnp.int32, (SUBLANES, t), 0)
    pair_id = jnp.concatenate([r16] + [k1 * PEER_TOPK + r8 for k1 in range(1, SUBLANES)]
                              + [(r8 + SUBLANES) * PEER_TOPK], axis=0)
    i1_parts, i2_parts, g_parts = [], [], []
    for h in range(PEER_HEADS):
        tops = []
        for p in range(2):
            c0 = (h * 2 + p) * PEER_N_KEYS
            q_hp = q_ref[:, c0:c0 + PEER_N_KEYS]
            s_t = lax.dot_general(keys_ref[h, p], q_hp, NT, precision=HI, preferred_element_type=F32)
            tops.append(_top16_rows(s_t, key_id, PEER_N_KEYS))
        (v0, x0), (v1, x1) = tops
        cand = jnp.concatenate([v0[0:1, :] + v1]
                               + [v0[k1:k1 + 1, :] + v1[0:SUBLANES, :] for k1 in range(1, SUBLANES)]
                               + [v0[SUBLANES:PEER_TOPK, :] + v1[0:1, :]], axis=0)
        best, pos = _top16_rows(cand, pair_id, PEER_TOPK * PEER_TOPK)
        k1 = pos >> 4
        k2 = pos & (PEER_TOPK - 1)
        a = jnp.zeros(pos.shape, jnp.int32)
        b = jnp.zeros(pos.shape, jnp.int32)
        for k in range(PEER_TOPK):
            a = jnp.where(k1 == k, x0[k:k + 1, :], a)
            b = jnp.where(k2 == k, x1[k:k + 1, :], b)
        e = jnp.exp(best - jnp.max(best, axis=0, keepdims=True))
        g_parts.append(e / jnp.sum(e, axis=0, keepdims=True))
        i1_parts.append(a)
        i2_parts.append(b)
    i1_ref[...] = jnp.concatenate(i1_parts, axis=0).T
    i2_ref[...] = jnp.concatenate(i2_parts, axis=0).T
    g_ref[...] = jnp.concatenate(g_parts, axis=0).T


def _route(q, keys):
    t = q.shape[0]
    tb = min(4 * LANES, t)
    slots = PEER_HEADS * PEER_TOPK
    spec = pl.BlockSpec((tb, slots), lambda i: (i, 0))
    return pl.pallas_call(
        _route_kernel,
        grid=(t // tb,),
        in_specs=[pl.BlockSpec((tb, q.shape[1]), lambda i: (i, 0)),
                  pl.BlockSpec(keys.shape, lambda i: (0, 0, 0, 0))],
        out_specs=[spec, spec, spec],
        out_shape=[jax.ShapeDtypeStruct((t, slots), jnp.int32), jax.ShapeDtypeStruct((t, slots), jnp.int32),
                   jax.ShapeDtypeStruct((t, slots), F32)],
        compiler_params=_cparams(("parallel",), 32),
        name="peer_route",
    )(q, keys)


def _expert_kernel(x_ref, h_ref, i1_ref, i2_ref, g_ref, gf_ref, u_ref, v_ref, o_ref, w_scr, acc_scr, *,
                   final_norm):
    c = pl.program_id(1)
    tb = x_ref.shape[0]
    n_sub = PEER_EXPERT_CHUNK // PEER_N_KEYS
    w_stride = tb + W_PAD

    @pl.when(c == 0)
    def _():
        sub = lax.broadcasted_iota(jnp.int32, (PEER_N_KEYS, PEER_N_KEYS), 0)

        def build(t, carry):
            i1_row = i1_ref[pl.ds(t, 1), :]
            i2_row = i2_ref[pl.ds(t, 1), :]
            g_row = g_ref[pl.ds(t, 1), :]
            p_t = jnp.where(sub == i1_row, g_row, 0.0).astype(BF16)
            q_t = jnp.where(sub == i2_row, 1.0, 0.0).astype(BF16)
            w_t = lax.dot_general(p_t, q_t, NT, preferred_element_type=F32)
            w_scr[pl.ds(t, PEER_N_KEYS, stride=w_stride), :] = w_t
            return carry

        lax.fori_loop(0, tb, build, 0, unroll=32)
        acc_scr[...] = jnp.zeros(acc_scr.shape, F32)

    act = lax.dot_general(h_ref[...], u_ref[...], NT, preferred_element_type=F32)
    gel = 0.5 * act * (1.0 + lax.erf(act * math.sqrt(0.5)))
    parts = []
    for a in range(n_sub):
        w_a = w_scr[pl.ds(pl.multiple_of((c * n_sub + a) * w_stride, SUBLANES), tb), :]
        parts.append((w_a * gel[:, a * PEER_N_KEYS:(a + 1) * PEER_N_KEYS]).astype(BF16))
    wa = jnp.concatenate(parts, axis=1)
    acc_scr[...] += jnp.dot(wa, v_ref[...], preferred_element_type=F32)

    @pl.when(c == pl.num_programs(1) - 1)
    def _():
        y = x_ref[...] + acc_scr[...]
        if final_norm:
            ms = jnp.mean(y * y, axis=-1, keepdims=True)
            y = y * lax.rsqrt(ms + EPS) * gf_ref[...]
        o_ref[...] = y


def _experts(x, h, i1, i2, g, g_final, u, v, *, tb, final_norm):
    t = x.shape[0]
    slots = i1.shape[1]
    row = lambda w: pl.BlockSpec((tb, w), lambda i, c: (i, 0))
    chunk = pl.BlockSpec((PEER_EXPERT_CHUNK, D_MODEL), lambda i, c: (c, 0))
    return pl.pallas_call(
        functools.partial(_expert_kernel, final_norm=final_norm),
        grid=(t // tb, PEER_EXPERTS // PEER_EXPERT_CHUNK),
        in_specs=[row(D_MODEL), row(D_MODEL), row(slots), row(slots), row(slots),
                  pl.BlockSpec((1, D_MODEL), lambda i, c: (0, 0)), chunk, chunk],
        out_specs=row(D_MODEL),
        out_shape=jax.ShapeDtypeStruct((t, D_MODEL), F32),
        scratch_shapes=[pltpu.VMEM((PEER_N_KEYS * (tb + W_PAD), PEER_N_KEYS), F32),
                        pltpu.VMEM((tb, D_MODEL), F32)],
        compiler_params=_cparams(("parallel", "arbitrary"), 56),
        name="peer_experts",
    )(x, h, i1, i2, g, g_final.reshape(1, D_MODEL), u, v)


def _layer_tail(x, conv_o, moba_o, diff_o, lw, g_final, *, tm, tb, final_norm):
    x1 = _out_proj(x, conv_o, moba_o, diff_o, lw["wo_c"], lw["wo_m"], lw["wo_d"], tm=min(tm, 256))
    q, h2 = _project(x1, lw["g_ffn"], lw["wq"], emit_h=True, tm=tm, tn=1024)
    i1, i2, g = _route(q, lw["keys"])
    return _experts(x1, h2, i1, i2, g, g_final, lw["u"], lw["v"], tb=tb, final_norm=final_norm)


def kernel(x_prompt, x_sample, cache_moba_k, cache_moba_v, cache_diff_k, cache_diff_v, state_conv, page_table,
           g_mix, w_in, conv_w, diff_lambda_q1, diff_lambda_k1, diff_lambda_q2, diff_lambda_k2, diff_subln_g,
           w_out, g_ffn, peer_w_query, peer_sub_keys, peer_expert_u, peer_expert_v, g_final):
    xp = x_prompt.reshape(N_PROMPT, D_MODEL)
    xs = x_sample.reshape(N_SAMPLE, D_MODEL)
    cache_moba_k, cache_moba_v, cache_diff_k, cache_diff_v = (
        _head_major(c) for c in (cache_moba_k, cache_moba_v, cache_diff_k, cache_diff_v))
    prompt_state, sample_state = [], []
    for l in range(DEPTH):
        lam_init = 0.8 - 0.6 * math.exp(-0.3 * l)
        final = l == DEPTH - 1
        wo = w_out[l].astype(BF16)
        lw = dict(wo_c=wo[:CONV_CH], wo_m=wo[CONV_CH:CONV_CH + MOBA_HEADS * HEAD_DIM],
                  wo_d=wo[CONV_CH + MOBA_HEADS * HEAD_DIM:], g_ffn=g_ffn[l], wq=peer_w_query[l].astype(BF16),
                  keys=peer_sub_keys[l], u=peer_expert_u[l].astype(BF16), v=peer_expert_v[l].astype(BF16))
        w_in_b = w_in[l].astype(BF16)
        lams = [a[l].reshape(1, DIFF_QK_DIM) for a in
                (diff_lambda_q1, diff_lambda_k1, diff_lambda_q2, diff_lambda_k2)]

        proj, *new_kv = _project(xp, g_mix[l], w_in_b, seq=SEQ, tm=512, tn=STATE_WIDTH)
        conv_o, conv_new = _conv_prompt(proj, conv_w[l])
        moba_o = _moba_prompt(proj)
        diff_o = _diff_prompt(proj, lams, diff_subln_g[l], lam_init)
        xp = _layer_tail(xp, conv_o, moba_o, diff_o, lw, g_final, tm=512, tb=256, final_norm=final)
        prompt_state.append((new_kv, conv_new))

        proj_s = _project(xs, g_mix[l], w_in_b, tm=N_SAMPLE, tn=1024)
        proj3 = proj_s.reshape(DEC_BATCH, DEC_SEQ, proj_s.shape[1])
        conv_o, conv_new = _conv_sample(proj_s, conv_w[l], state_conv[l])
        moba_o = _moba_sample(proj3, cache_moba_k, cache_moba_v, page_table, l)
        diff_o = _diff_sample(proj3, cache_diff_k, cache_diff_v, page_table, l, lams, diff_subln_g[l], lam_init)
        xs = _layer_tail(xs, conv_o, moba_o.reshape(N_SAMPLE, -1), diff_o.reshape(N_SAMPLE, -1), lw, g_final,
                         tm=N_SAMPLE, tb=N_SAMPLE, final_norm=final)
        sample_state.append((proj_s, conv_new))

    def sample_states(group):
        cols = lambda c0: jnp.stack([p[:, c0 * LANES:(c0 + MOBA_HEADS) * LANES]
                                     .reshape(DEC_BATCH, DEC_SEQ, MOBA_HEADS, HEAD_DIM) for p, _ in group])
        return cols(COL_MK), cols(COL_MV), cols(COL_DK), cols(COL_DV), jnp.stack([c for _, c in group])

    def prompt_states(group):
        kv = tuple(jnp.transpose(jnp.stack([s[n] for s, _ in group]), (0, 1, 3, 2, 4))
                   for n in range(len(STATE_TILES)))
        return kv + (jnp.stack([c for _, c in group]),)

    y_prompt = xp.reshape(BATCH, SEQ, D_MODEL)
    y_sample = xs.reshape(DEC_BATCH, DEC_SEQ, D_MODEL)
    return (y_prompt, y_sample) + prompt_states(prompt_state) + sample_states(sample_state)
```

```python
import functools
import math

import jax
import jax.numpy as jnp
from jax import lax
from jax.experimental import pallas as pl
from jax.experimental.pallas import tpu as pltpu

D_MODEL = 2048
BATCH = 4
SEQ = 2048
DEPTH = 2
DEC_BATCH = 32
DEC_SEQ = 4
PAST_LEN = 8192
PAGE_SIZE = 128
HEAD_DIM = 128
CONV_CH = 512
CONV_WIDTH = 3
MOBA_HEADS = 6
MOBA_BLOCK = 256
MOBA_TOPK = 3
DIFF_HEADS = 6
DIFF_QK_DIM = 64
PEER_HEADS = 8
PEER_N_KEYS = 128
PEER_EXPERTS = PEER_N_KEYS * PEER_N_KEYS
PEER_TOPK = 16
EPS = 1e-6
NEG = -1e30

N_PAGES = PAST_LEN // PAGE_SIZE
N_SAMPLE = DEC_BATCH * DEC_SEQ
N_PROMPT = BATCH * SEQ
N_PROMPT_BLOCKS = SEQ // MOBA_BLOCK
N_PAST_BLOCKS = PAST_LEN // MOBA_BLOCK

COL_MQ, COL_MK, COL_MV = 12, 18, 24
COL_DQ, COL_DK, COL_DV = 30, 36, 42

LANES = 128
SUBLANES = 8
MIB = 1024 * 1024

PAGES_PER_STEP = 16
N_PAGE_STEPS = N_PAGES // PAGES_PER_STEP
KEYS_PER_STEP = PAGES_PER_STEP * PAGE_SIZE
BLOCKS_PER_STEP = KEYS_PER_STEP // MOBA_BLOCK

PEER_EXPERT_CHUNK = 1024
W_PAD = SUBLANES

F32 = jnp.float32
BF16 = jnp.bfloat16
HI = lax.Precision.HIGHEST
NT = (((1,), (1,)), ((), ()))


def _cparams(sem, vmem_mib):
    return pltpu.CompilerParams(dimension_semantics=sem, vmem_limit_bytes=vmem_mib * MIB)


STATE_WIDTH = MOBA_HEADS * HEAD_DIM
STATE_TILES = tuple(c // MOBA_HEADS for c in (COL_MK, COL_MV, COL_DK, COL_DV))


def _proj_kernel(x_ref, g_ref, w_ref, o_ref, *rest, emit_h, emit_states):
    rest = list(rest)
    hout_ref = rest.pop(0) if emit_h else None
    state_refs = [rest.pop(0) for _ in STATE_TILES] if emit_states else []
    h_scr = rest.pop(0)
    j = pl.program_id(1)

    @pl.when(j == 0)
    def _():
        x = x_ref[...]
        ms = jnp.mean(x * x, axis=-1, keepdims=True)
        hb = (x * lax.rsqrt(ms + EPS) * g_ref[...]).astype(BF16)
        h_scr[...] = hb
        if emit_h:
            hout_ref[...] = hb

    o_ref[...] = jnp.dot(h_scr[...], w_ref[...], preferred_element_type=F32)

    for tile, s_ref in zip(STATE_TILES, state_refs):
        @pl.when(j == tile)
        def _(s_ref=s_ref):
            for h in range(MOBA_HEADS):
                s_ref[h] = o_ref[:, h * HEAD_DIM:(h + 1) * HEAD_DIM]


def _project(x, g, w, layer, *, emit_h=False, seq=None, tm, tn):
    t, d = x.shape
    n = w.shape[2]
    emit_states = seq is not None
    out_shape = [jax.ShapeDtypeStruct((t, n), F32)]
    out_specs = [pl.BlockSpec((tm, tn), lambda i, j: (i, j))]
    if emit_h:
        out_shape.append(jax.ShapeDtypeStruct((t, d), BF16))
        out_specs.append(pl.BlockSpec((tm, d), lambda i, j: (i, 0)))
    if emit_states:
        assert tn == STATE_WIDTH and seq % tm == 0
        per_seq = seq // tm
        for _ in STATE_TILES:
            out_shape.append(jax.ShapeDtypeStruct((t // seq, MOBA_HEADS, seq, HEAD_DIM), F32))
            out_specs.append(pl.BlockSpec((None, MOBA_HEADS, tm, HEAD_DIM),
                                          lambda i, j: (i // per_seq, 0, i % per_seq, 0)))
    res = pl.pallas_call(
        functools.partial(_proj_kernel, emit_h=emit_h, emit_states=emit_states),
        grid=(t // tm, n // tn),
        in_specs=[pl.BlockSpec((tm, d), lambda i, j: (i, 0)),
                  pl.BlockSpec((1, d), lambda i, j: (0, 0)),
                  pl.BlockSpec((None, d, tn), lambda i, j: (layer, 0, j))],
        out_specs=out_specs, out_shape=out_shape,
        scratch_shapes=[pltpu.VMEM((tm, d), BF16)],
        compiler_params=_cparams(("parallel", "arbitrary"), 48),
        name="proj",
    )(x, g.reshape(1, d), w)
    return res if len(res) > 1 else res[0]


def _conv_prompt_kernel(cb_ref, cc_ref, cx_ref, w_ref, y_ref, st_ref, u_scr):
    t = cb_ref.shape[0]
    u = cc_ref[...] * cx_ref[...]
    u_scr[0:SUBLANES, :] = jnp.zeros((SUBLANES, CONV_CH), F32)
    u_scr[SUBLANES:SUBLANES + t, :] = u
    w = w_ref[...]
    y = w[0:1, :] * u_scr[SUBLANES - 2:SUBLANES - 2 + t, :]
    y = y + w[1:2, :] * u_scr[SUBLANES - 1:SUBLANES - 1 + t, :]
    y = y + w[2:3, :] * u
    y_ref[...] = cb_ref[...] * y
    st_ref[...] = u_scr[SUBLANES + t - 2:SUBLANES + t, :]


def _conv_prompt(proj, conv_w):
    return pl.pallas_call(
        _conv_prompt_kernel,
        grid=(BATCH,),
        in_specs=[pl.BlockSpec((SEQ, CONV_CH), lambda b: (b, 0)),
                  pl.BlockSpec((SEQ, CONV_CH), lambda b: (b, 1)),
                  pl.BlockSpec((SEQ, CONV_CH), lambda b: (b, 2)),
                  pl.BlockSpec((CONV_WIDTH, CONV_CH), lambda b: (0, 0))],
        out_specs=[pl.BlockSpec((SEQ, CONV_CH), lambda b: (b, 0)),
                   pl.BlockSpec((None, CONV_WIDTH - 1, CONV_CH), lambda b: (b, 0, 0))],
        out_shape=[jax.ShapeDtypeStruct((N_PROMPT, CONV_CH), F32),
                   jax.ShapeDtypeStruct((BATCH, CONV_WIDTH - 1, CONV_CH), F32)],
        scratch_shapes=[pltpu.VMEM((SEQ + SUBLANES, CONV_CH), F32)],
        compiler_params=_cparams(("parallel",), 48),
        name="conv_prompt",
    )(proj, proj, proj, conv_w)


def _conv_sample_kernel(cb_ref, cc_ref, cx_ref, w_ref, st_in_ref, y_ref, st_ref, u_scr):
    w = w_ref[...]
    for s in range(DEC_BATCH):
        r = s * DEC_SEQ
        u_scr[0:2, :] = st_in_ref[s]
        u_scr[2:2 + DEC_SEQ, :] = cc_ref[r:r + DEC_SEQ, :] * cx_ref[r:r + DEC_SEQ, :]
        y = w[0:1, :] * u_scr[0:DEC_SEQ, :]
        y = y + w[1:2, :] * u_scr[1:1 + DEC_SEQ, :]
        y = y + w[2:3, :] * u_scr[2:2 + DEC_SEQ, :]
        y_ref[r:r + DEC_SEQ, :] = cb_ref[r:r + DEC_SEQ, :] * y
        st_ref[s] = u_scr[DEC_SEQ:DEC_SEQ + 2, :]


def _conv_sample(proj, conv_w, state):
    return pl.pallas_call(
        _conv_sample_kernel,
        grid=(1,),
        in_specs=[pl.BlockSpec((N_SAMPLE, CONV_CH), lambda i: (0, 0)),
                  pl.BlockSpec((N_SAMPLE, CONV_CH), lambda i: (0, 1)),
                  pl.BlockSpec((N_SAMPLE, CONV_CH), lambda i: (0, 2)),
                  pl.BlockSpec((CONV_WIDTH, CONV_CH), lambda i: (0, 0)),
                  pl.BlockSpec((DEC_BATCH, CONV_WIDTH - 1, CONV_CH), lambda i: (0, 0, 0))],
        out_specs=[pl.BlockSpec((N_SAMPLE, CONV_CH), lambda i: (0, 0)),
                   pl.BlockSpec((DEC_BATCH, CONV_WIDTH - 1, CONV_CH), lambda i: (0, 0, 0))],
        out_shape=[jax.ShapeDtypeStruct((N_SAMPLE, CONV_CH), F32),
                   jax.ShapeDtypeStruct((DEC_BATCH, CONV_WIDTH - 1, CONV_CH), F32)],
        scratch_shapes=[pltpu.VMEM((SUBLANES, CONV_CH), F32)],
        name="conv_sample",
    )(proj, proj, proj, conv_w, state)


def _rank_lt(gate, n_blocks, k, axis):
    block_id = lax.broadcasted_iota(jnp.int32, gate.shape, axis)
    rank = jnp.zeros(gate.shape, jnp.int32)
    for n2 in range(n_blocks):
        other = gate[n2:n2 + 1, :] if axis == 0 else gate[:, n2:n2 + 1]
        beats = jnp.where(other > gate, 1, jnp.where(other == gate, jnp.where(block_id > n2, 1, 0), 0))
        rank = rank + beats
    return jnp.where(rank < k, 1, 0)


def _for_past_block_pairs(i, update):
    for a in range(0, N_PROMPT_BLOCKS - 1, 2):
        b = a + 1
        if b < N_PROMPT_BLOCKS - 1:
            pl.when(b < i)(functools.partial(update, (a, b)))
        pl.when(b == i)(functools.partial(update, (a,)))


HEADS_PER_STEP = 2


def _head_cols(hh):
    return slice(hh * HEAD_DIM, (hh + 1) * HEAD_DIM)


def _stage_kv(k_ref, v_ref, kb_scr, vt_scr):
    for hh in range(HEADS_PER_STEP):
        kb_scr[hh] = k_ref[:, _head_cols(hh)].astype(BF16)
        for n in range(N_PROMPT_BLOCKS):
            vt_scr[hh, n] = v_ref[n * MOBA_BLOCK:(n + 1) * MOBA_BLOCK, _head_cols(hh)].T.astype(BF16)


def _prompt_specs(col_q, col_k, col_v):
    tq, width = MOBA_BLOCK, HEADS_PER_STEP * HEAD_DIM
    q_spec = pl.BlockSpec((tq, width), lambda b, g, i: (b * N_PROMPT_BLOCKS + i, col_q // HEADS_PER_STEP + g))
    k_spec = pl.BlockSpec((SEQ, width), lambda b, g, i: (b, col_k // HEADS_PER_STEP + g))
    v_spec = pl.BlockSpec((SEQ, width), lambda b, g, i: (b, col_v // HEADS_PER_STEP + g))
    o_spec = pl.BlockSpec((tq, width), lambda b, g, i: (b * N_PROMPT_BLOCKS + i, g))
    return q_spec, k_spec, v_spec, o_spec


def _moba_prompt_kernel(q_ref, k_ref, v_ref, o_ref, kb_scr, vt_scr, km_scr, m_scr, l_scr, acc_scr):
    i = pl.program_id(2)
    tq = MOBA_BLOCK
    scale = HEAD_DIM ** -0.5

    @pl.when(i == 0)
    def _():
        _stage_kv(k_ref, v_ref, kb_scr, vt_scr)
        for hh in range(HEADS_PER_STEP):
            k = k_ref[:, _head_cols(hh)]
            km_scr[hh] = jnp.sum(k.reshape(N_PROMPT_BLOCKS, MOBA_BLOCK, HEAD_DIM), axis=1) * (1.0 / MOBA_BLOCK)

    own = pl.multiple_of(i * tq, tq)
    key = lax.broadcasted_iota(jnp.int32, (tq, tq), 0)
    qry = lax.broadcasted_iota(jnp.int32, (tq, tq), 1)
    sels, qbs = [], []
    for hh in range(HEADS_PER_STEP):
        q = q_ref[:, _head_cols(hh)]
        gate = lax.dot_general(km_scr[hh], q, NT, precision=HI, preferred_element_type=F32)
        blk = lax.broadcasted_iota(jnp.int32, gate.shape, 0)
        gate = jnp.where(blk < i, gate, NEG)
        sels.append(_rank_lt(gate, N_PROMPT_BLOCKS, MOBA_TOPK, axis=0) * jnp.where(blk < i, 1, 0))
        qb = q.astype(BF16)
        qbs.append(qb)
        s = lax.dot_general(kb_scr[hh, pl.ds(own, tq), :], qb, NT, preferred_element_type=F32) * scale
        s = jnp.where(key <= qry, s, NEG)
        m = jnp.max(s, axis=0, keepdims=True)
        p = jnp.exp(s - m)
        m_scr[hh] = m
        l_scr[hh] = jnp.sum(p, axis=0, keepdims=True)
        acc_scr[hh] = jnp.dot(vt_scr[hh, i], p.astype(BF16), preferred_element_type=F32)

    def past_update(blocks):
        for hh in range(HEADS_PER_STEP):
            scores = []
            for n in blocks:
                sn = lax.dot_general(kb_scr[hh, n * tq:(n + 1) * tq, :], qbs[hh], NT,
                                     preferred_element_type=F32) * scale
                scores.append(jnp.where(sels[hh][n:n + 1, :] > 0, sn, NEG))
            m_old = m_scr[hh]
            m_new = m_old
            for sn in scores:
                m_new = jnp.maximum(m_new, jnp.max(sn, axis=0, keepdims=True))
            alpha = jnp.exp(m_old - m_new)
            l = alpha * l_scr[hh]
            acc = alpha * acc_scr[hh]
            for n, sn in zip(blocks, scores):
                pn = jnp.exp(sn - m_new)
                l = l + jnp.sum(pn, axis=0, keepdims=True)
                acc = acc + jnp.dot(vt_scr[hh, n], pn.astype(BF16), preferred_element_type=F32)
            l_scr[hh] = l
            acc_scr[hh] = acc
            m_scr[hh] = m_new

    _for_past_block_pairs(i, past_update)
    for hh in range(HEADS_PER_STEP):
        o_ref[:, _head_cols(hh)] = (acc_scr[hh] / l_scr[hh]).T


def _moba_prompt(proj):
    tq, hps = MOBA_BLOCK, HEADS_PER_STEP
    q_spec, k_spec, v_spec, o_spec = _prompt_specs(COL_MQ, COL_MK, COL_MV)
    return pl.pallas_call(
        _moba_prompt_kernel,
        grid=(BATCH, MOBA_HEADS // hps, N_PROMPT_BLOCKS),
        in_specs=[q_spec, k_spec, v_spec],
        out_specs=o_spec,
        out_shape=jax.ShapeDtypeStruct((N_PROMPT, MOBA_HEADS * HEAD_DIM), F32),
        scratch_shapes=[pltpu.VMEM((hps, SEQ, HEAD_DIM), BF16),
                        pltpu.VMEM((hps, N_PROMPT_BLOCKS, HEAD_DIM, tq), BF16),
                        pltpu.VMEM((hps, N_PROMPT_BLOCKS, HEAD_DIM), F32),
                        pltpu.VMEM((hps, 1, tq), F32), pltpu.VMEM((hps, 1, tq), F32),
                        pltpu.VMEM((hps, HEAD_DIM, tq), F32)],
        compiler_params=_cparams(("parallel", "parallel", "arbitrary"), 32),
        name="moba_prompt",
    )(proj, proj, proj)


def _diff_lambda(lq1_ref, lk1_ref, lq2_ref, lk2_ref, lam_init):
    a = jnp.sum(lq1_ref[...] * lk1_ref[...], axis=-1, keepdims=True)
    b = jnp.sum(lq2_ref[...] * lk2_ref[...], axis=-1, keepdims=True)
    return jnp.exp(a) - jnp.exp(b) + lam_init


def _diff_finish(o1, o2, lam, g, lam_init, axis=-1):
    o = o1 - lam * o2
    ms = jnp.mean(o * o, axis=axis, keepdims=True)
    return o * lax.rsqrt(ms + EPS) * g * (1.0 - lam_init)


def _split_maps(q):
    lane = lax.broadcasted_iota(jnp.int32, q.shape, 1)
    return jnp.where(lane < DIFF_QK_DIM, q, 0.0), jnp.where(lane >= DIFF_QK_DIM, q, 0.0)


def _diff_prompt_kernel(lq1_ref, lk1_ref, lq2_ref, lk2_ref, g_ref, q_ref, k_ref, v_ref, o_ref,
                        kb_scr, vt_scr, m_scr, l_scr, acc_scr, *, lam_init):
    i = pl.program_id(2)
    tq = MOBA_BLOCK
    scale = DIFF_QK_DIM ** -0.5

    @pl.when(i == 0)
    def _():
        _stage_kv(k_ref, v_ref, kb_scr, vt_scr)

    own = pl.multiple_of(i * tq, tq)
    key = lax.broadcasted_iota(jnp.int32, (tq, tq), 0)
    qry = lax.broadcasted_iota(jnp.int32, (tq, tq), 1)
    chains = []
    for hh in range(HEADS_PER_STEP):
        for c, qc in enumerate(_split_maps(q_ref[:, _head_cols(hh)])):
            chains.append((hh, 2 * hh + c, qc.astype(BF16)))
    for hh, slot, qc in chains:
        s = lax.dot_general(kb_scr[hh, pl.ds(own, tq), :], qc, NT, preferred_element_type=F32) * scale
        s = jnp.where(key <= qry, s, NEG)
        m = jnp.max(s, axis=0, keepdims=True)
        p = jnp.exp(s - m)
        m_scr[slot] = m
        l_scr[slot] = jnp.sum(p, axis=0, keepdims=True)
        acc_scr[slot] = jnp.dot(vt_scr[hh, i], p.astype(BF16), preferred_element_type=F32)

    def past_update(blocks):
        for hh, slot, qc in chains:
            scores = [lax.dot_general(kb_scr[hh, n * tq:(n + 1) * tq, :], qc, NT,
                                      preferred_element_type=F32) * scale for n in blocks]
            m_old = m_scr[slot]
            m_new = m_old
            for sn in scores:
                m_new = jnp.maximum(m_new, jnp.max(sn, axis=0, keepdims=True))
            alpha = jnp.exp(m_old - m_new)
            l = alpha * l_scr[slot]
            acc = alpha * acc_scr[slot]
            for n, sn in zip(blocks, scores):
                pn = jnp.exp(sn - m_new)
                l = l + jnp.sum(pn, axis=0, keepdims=True)
                acc = acc + jnp.dot(vt_scr[hh, n], pn.astype(BF16), preferred_element_type=F32)
            l_scr[slot] = l
            acc_scr[slot] = acc
            m_scr[slot] = m_new

    _for_past_block_pairs(i, past_update)
    lam = _diff_lambda(lq1_ref, lk1_ref, lq2_ref, lk2_ref, lam_init)
    for hh in range(HEADS_PER_STEP):
        y = _diff_finish(acc_scr[2 * hh] / l_scr[2 * hh], acc_scr[2 * hh + 1] / l_scr[2 * hh + 1], lam,
                         g_ref[...], lam_init, axis=0)
        o_ref[:, _head_cols(hh)] = y.T


def _diff_prompt(proj, lams, subln_g, lam_init):
    tq, hps = MOBA_BLOCK, HEADS_PER_STEP
    lam_specs = [pl.BlockSpec((1, DIFF_QK_DIM), lambda b, h, i: (0, 0))] * 4
    q_spec, k_spec, v_spec, o_spec = _prompt_specs(COL_DQ, COL_DK, COL_DV)
    return pl.pallas_call(
        functools.partial(_diff_prompt_kernel, lam_init=lam_init),
        grid=(BATCH, DIFF_HEADS // hps, N_PROMPT_BLOCKS),
        in_specs=lam_specs + [pl.BlockSpec((HEAD_DIM, 1), lambda b, h, i: (0, 0)), q_spec, k_spec, v_spec],
        out_specs=o_spec,
        out_shape=jax.ShapeDtypeStruct((N_PROMPT, DIFF_HEADS * HEAD_DIM), F32),
        scratch_shapes=[pltpu.VMEM((hps, SEQ, HEAD_DIM), BF16),
                        pltpu.VMEM((hps, N_PROMPT_BLOCKS, HEAD_DIM, tq), BF16),
                        pltpu.VMEM((2 * hps, 1, tq), F32), pltpu.VMEM((2 * hps, 1, tq), F32),
                        pltpu.VMEM((2 * hps, HEAD_DIM, tq), F32)],
        compiler_params=_cparams(("parallel", "parallel", "arbitrary"), 32),
        name="diff_prompt",
    )(*lams, subln_g.reshape(HEAD_DIM, 1), proj, proj, proj)


def _page_specs(layer, pinned_phase=None):
    specs = []
    for r in range(PAGES_PER_STEP):
        if pinned_phase is None:
            def imap(b, s, pt, r=r):
                return (layer, pt[b, s * PAGES_PER_STEP + r], 0, 0, 0)
        elif pinned_phase == 1:
            def imap(b, ph, s, pt, r=r):
                st = jnp.where(ph == 0, s, N_PAGE_STEPS - 1)
                return (layer, pt[b, st * PAGES_PER_STEP + r], 0, 0, 0)
        else:
            def imap(b, ph, s, pt, r=r):
                st = jnp.where(ph == 0, 0, s)
                return (layer, pt[b, st * PAGES_PER_STEP + r], 0, 0, 0)
        specs.append(pl.BlockSpec((None, None, MOBA_HEADS, PAGE_SIZE, HEAD_DIM), imap))
    return specs


def _head_major(cache):
    return jnp.transpose(cache, (0, 1, 3, 2, 4))


def _load_pages(refs, h):
    return jnp.concatenate([r[h] for r in refs], axis=0)


def _new_key_mask():
    r = lax.broadcasted_iota(jnp.int32, (SUBLANES, PAGE_SIZE), 0)
    j = lax.broadcasted_iota(jnp.int32, (SUBLANES, PAGE_SIZE), 1)
    return (j <= (r & (DEC_SEQ - 1))) & (j < DEC_SEQ)


def _stage_new_kv(kn_ref, vn_ref, knp_scr, vnp_scr, h):
    knp_scr[...] = jnp.zeros(knp_scr.shape, F32)
    vnp_scr[...] = jnp.zeros(vnp_scr.shape, F32)
    knp_scr[0:DEC_SEQ, :] = kn_ref[:, h * HEAD_DIM:(h + 1) * HEAD_DIM]
    vnp_scr[0:DEC_SEQ, :] = vn_ref[:, h * HEAD_DIM:(h + 1) * HEAD_DIM]


def _diff_sample_kernel(pt_ref, lq1_ref, lk1_ref, lq2_ref, lk2_ref, g_ref, q_ref, kn_ref, vn_ref, *rest,
                        lam_init):
    k_refs = rest[:PAGES_PER_STEP]
    v_refs = rest[PAGES_PER_STEP:2 * PAGES_PER_STEP]
    o_ref = rest[2 * PAGES_PER_STEP]
    q8_scr, knp_scr, vnp_scr, m_scr, l_scr, acc_scr = rest[2 * PAGES_PER_STEP + 1:]
    step = pl.program_id(1)
    scale = DIFF_QK_DIM ** -0.5

    @pl.when(step == 0)
    def _():
        for h in range(DIFF_HEADS):
            q1, q2 = _split_maps(q_ref[:, h * HEAD_DIM:(h + 1) * HEAD_DIM])
            q8_scr[h, 0:DEC_SEQ, :] = q1
            q8_scr[h, DEC_SEQ:2 * DEC_SEQ, :] = q2
        m_scr[...] = jnp.full(m_scr.shape, -jnp.inf, F32)
        l_scr[...] = jnp.zeros(l_scr.shape, F32)
        acc_scr[...] = jnp.zeros(acc_scr.shape, F32)

    def update(h, s, vb):
        m_old = m_scr[h]
        m_new = jnp.maximum(m_old, jnp.max(s, axis=-1, keepdims=True))
        alpha = jnp.exp(m_old - m_new)
        p = jnp.exp(s - m_new)
        l_scr[h] = alpha * l_scr[h] + jnp.sum(p, axis=-1, keepdims=True)
        acc_scr[h] = alpha * acc_scr[h] + jnp.dot(p.astype(BF16), vb, preferred_element_type=F32)
        m_scr[h] = m_new

    for h in range(DIFF_HEADS):
        q8 = q8_scr[h].astype(BF16)
        kb = _load_pages(k_refs, h).astype(BF16)
        vb = _load_pages(v_refs, h).astype(BF16)
        update(h, lax.dot_general(q8, kb, NT, preferred_element_type=F32) * scale, vb)

    @pl.when(step == N_PAGE_STEPS - 1)
    def _():
        lam = _diff_lambda(lq1_ref, lk1_ref, lq2_ref, lk2_ref, lam_init)
        for h in range(DIFF_HEADS):
            q8 = q8_scr[h].astype(BF16)
            _stage_new_kv(kn_ref, vn_ref, knp_scr, vnp_scr, h)
            sn = lax.dot_general(q8, knp_scr[...].astype(BF16), NT, preferred_element_type=F32) * scale
            update(h, jnp.where(_new_key_mask(), sn, NEG), vnp_scr[...].astype(BF16))
            o = acc_scr[h] / l_scr[h]
            o_ref[:, h * HEAD_DIM:(h + 1) * HEAD_DIM] = _diff_finish(
                o[0:DEC_SEQ, :], o[DEC_SEQ:2 * DEC_SEQ, :], lam, g_ref[...], lam_init)


def _tok_spec(c0):
    width = MOBA_HEADS * HEAD_DIM
    return pl.BlockSpec((None, DEC_SEQ, width), lambda b, *_: (b, 0, c0 // MOBA_HEADS))


def _diff_sample(proj3, cache_k, cache_v, page_table, layer, lams, subln_g, lam_init):
    lam_specs = [pl.BlockSpec((1, DIFF_QK_DIM), lambda b, s, pt: (0, 0))] * 4
    grid_spec = pltpu.PrefetchScalarGridSpec(
        num_scalar_prefetch=1,
        grid=(DEC_BATCH, N_PAGE_STEPS),
        in_specs=lam_specs + [pl.BlockSpec((1, HEAD_DIM), lambda b, s, pt: (0, 0)),
                              _tok_spec(COL_DQ), _tok_spec(COL_DK), _tok_spec(COL_DV)]
        + _page_specs(layer) + _page_specs(layer),
        out_specs=pl.BlockSpec((None, DEC_SEQ, DIFF_HEADS * HEAD_DIM), lambda b, s, pt: (b, 0, 0)),
        scratch_shapes=[pltpu.VMEM((DIFF_HEADS, SUBLANES, HEAD_DIM), F32),
                        pltpu.VMEM((PAGE_SIZE, HEAD_DIM), F32), pltpu.VMEM((PAGE_SIZE, HEAD_DIM), F32),
                        pltpu.VMEM((DIFF_HEADS, SUBLANES, 1), F32), pltpu.VMEM((DIFF_HEADS, SUBLANES, 1), F32),
                        pltpu.VMEM((DIFF_HEADS, SUBLANES, HEAD_DIM), F32)])
    return pl.pallas_call(
        functools.partial(_diff_sample_kernel, lam_init=lam_init),
        grid_spec=grid_spec,
        out_shape=jax.ShapeDtypeStruct((DEC_BATCH, DEC_SEQ, DIFF_HEADS * HEAD_DIM), F32),
        compiler_params=_cparams(("parallel", "arbitrary"), 56),
        name="diff_sample",
    )(page_table, *lams, subln_g.reshape(1, HEAD_DIM), proj3, proj3, proj3,
      *([cache_k] * PAGES_PER_STEP), *([cache_v] * PAGES_PER_STEP))


def _moba_sample_kernel(pt_ref, q_ref, kn_ref, vn_ref, *rest):
    k_refs = rest[:PAGES_PER_STEP]
    v_refs = rest[PAGES_PER_STEP:2 * PAGES_PER_STEP]
    o_ref = rest[2 * PAGES_PER_STEP]
    q8_scr, knp_scr, vnp_scr, s_scr, km_scr, p_scr, l_scr, acc_scr = rest[2 * PAGES_PER_STEP + 1:]
    phase = pl.program_id(1)
    step = pl.program_id(2)
    scale = HEAD_DIM ** -0.5

    @pl.when((phase == 0) & (step == 0))
    def _():
        q8_scr[...] = jnp.zeros(q8_scr.shape, F32)
        for h in range(MOBA_HEADS):
            q8_scr[h, 0:DEC_SEQ, :] = q_ref[:, h * HEAD_DIM:(h + 1) * HEAD_DIM]

    @pl.when(phase == 0)
    def _():
        for h in range(MOBA_HEADS):
            k = _load_pages(k_refs, h)
            km_scr[h, step] = (jnp.sum(k.reshape(BLOCKS_PER_STEP, MOBA_BLOCK, HEAD_DIM), axis=1)
                               * (1.0 / MOBA_BLOCK))
            s_scr[h, step] = lax.dot_general(q8_scr[h].astype(BF16), k.astype(BF16), NT,
                                             preferred_element_type=F32) * scale

    @pl.when((phase == 1) & (step == 0))
    def _():
        for h in range(MOBA_HEADS):
            q8 = q8_scr[h]
            km = km_scr[h].reshape(N_PAST_BLOCKS, HEAD_DIM)
            gate = lax.dot_general(q8, km, NT, precision=HI, preferred_element_type=F32)
            sel = _rank_lt(gate, N_PAST_BLOCKS, MOBA_TOPK, axis=1)
            _stage_new_kv(kn_ref, vn_ref, knp_scr, vnp_scr, h)
            sn = lax.dot_general(q8.astype(BF16), knp_scr[...].astype(BF16), NT,
                                 preferred_element_type=F32) * scale
            sn = jnp.where(_new_key_mask(), sn, NEG)
            m = jnp.max(sn, axis=-1, keepdims=True)

            def block(n, h=h, sel=sel):
                st, jb = divmod(n, BLOCKS_PER_STEP)
                return jnp.where(sel[:, n:n + 1] > 0,
                                 s_scr[h, st, :, jb * MOBA_BLOCK:(jb + 1) * MOBA_BLOCK], NEG)

            for n in range(N_PAST_BLOCKS):
                m = jnp.maximum(m, jnp.max(block(n), axis=-1, keepdims=True))
            pn = jnp.exp(sn - m)
            l = jnp.sum(pn, axis=-1, keepdims=True)
            for n in range(N_PAST_BLOCKS):
                st, jb = divmod(n, BLOCKS_PER_STEP)
                pb = jnp.exp(block(n) - m)
                l = l + jnp.sum(pb, axis=-1, keepdims=True)
                p_scr[h, st, :, jb * MOBA_BLOCK:(jb + 1) * MOBA_BLOCK] = pb
            l_scr[h] = l
            acc_scr[h] = jnp.dot(pn.astype(BF16), vnp_scr[...].astype(BF16), preferred_element_type=F32)

    @pl.when(phase == 1)
    def _():
        for h in range(MOBA_HEADS):
            vb = _load_pages(v_refs, h).astype(BF16)
            acc_scr[h] += jnp.dot(p_scr[h, step].astype(BF16), vb, preferred_element_type=F32)

    @pl.when((phase == 1) & (step == N_PAGE_STEPS - 1))
    def _():
        for h in range(MOBA_HEADS):
            o = acc_scr[h] / l_scr[h]
            o_ref[:, h * HEAD_DIM:(h + 1) * HEAD_DIM] = o[0:DEC_SEQ, :]


def _moba_sample(proj3, cache_k, cache_v, page_table, layer):
    grid_spec = pltpu.PrefetchScalarGridSpec(
        num_scalar_prefetch=1,
        grid=(DEC_BATCH, 2, N_PAGE_STEPS),
        in_specs=[_tok_spec(COL_MQ), _tok_spec(COL_MK), _tok_spec(COL_MV)]
        + _page_specs(layer, pinned_phase=1) + _page_specs(layer, pinned_phase=0),
        out_specs=pl.BlockSpec((None, DEC_SEQ, MOBA_HEADS * HEAD_DIM), lambda b, ph, s, pt: (b, 0, 0)),
        scratch_shapes=[pltpu.VMEM((MOBA_HEADS, SUBLANES, HEAD_DIM), F32),
                        pltpu.VMEM((PAGE_SIZE, HEAD_DIM), F32), pltpu.VMEM((PAGE_SIZE, HEAD_DIM), F32),
                        pltpu.VMEM((MOBA_HEADS, N_PAGE_STEPS, SUBLANES, KEYS_PER_STEP), F32),
                        pltpu.VMEM((MOBA_HEADS, N_PAGE_STEPS, BLOCKS_PER_STEP, HEAD_DIM), F32),
                        pltpu.VMEM((MOBA_HEADS, N_PAGE_STEPS, SUBLANES, KEYS_PER_STEP), F32),
                        pltpu.VMEM((MOBA_HEADS, SUBLANES, 1), F32),
                        pltpu.VMEM((MOBA_HEADS, SUBLANES, HEAD_DIM), F32)])
    return pl.pallas_call(
        _moba_sample_kernel,
        grid_spec=grid_spec,
        out_shape=jax.ShapeDtypeStruct((DEC_BATCH, DEC_SEQ, MOBA_HEADS * HEAD_DIM), F32),
        compiler_params=_cparams(("parallel", "arbitrary", "arbitrary"), 56),
        name="moba_sample",
    )(page_table, proj3, proj3, proj3, *([cache_k] * PAGES_PER_STEP), *([cache_v] * PAGES_PER_STEP))


def _out_proj_kernel(x_ref, c_ref, m_ref, d_ref, wc_ref, wm_ref, wd_ref, o_ref):
    acc = jnp.dot(c_ref[...].astype(BF16), wc_ref[...], preferred_element_type=F32)
    acc += jnp.dot(m_ref[...].astype(BF16), wm_ref[...], preferred_element_type=F32)
    acc += jnp.dot(d_ref[...].astype(BF16), wd_ref[...], preferred_element_type=F32)
    o_ref[...] = x_ref[...] + acc


def _out_proj(x, conv_o, moba_o, diff_o, wc, wm, wd, tm):
    t = x.shape[0]
    row = lambda w: pl.BlockSpec((tm, w), lambda i: (i, 0))
    full = lambda a: pl.BlockSpec(a.shape, lambda i: (0, 0))
    return pl.pallas_call(
        _out_proj_kernel,
        grid=(t // tm,),
        in_specs=[row(D_MODEL), row(conv_o.shape[1]), row(moba_o.shape[1]), row(diff_o.shape[1]),
                  full(wc), full(wm), full(wd)],
        out_specs=row(D_MODEL),
        out_shape=jax.ShapeDtypeStruct((t, D_MODEL), F32),
        compiler_params=_cparams(("parallel",), 48),
        name="out_proj",
    )(x, conv_o, moba_o, diff_o, wc, wm, wd)


def _top16_rows(s, row_id, id_bound):
    t = s.shape[1]
    slot = lax.broadcasted_iota(jnp.int32, (PEER_TOPK, t), 0)

    def body(k, carry):
        s, vals, idxs = carry
        m = jnp.max(s, axis=0, keepdims=True)
        idx = jnp.min(jnp.where(s == m, row_id, id_bound), axis=0, keepdims=True)
        vals = jnp.where(slot == k, m, vals)
        idxs = jnp.where(slot == k, idx, idxs)
        s = jnp.where(row_id == idx, -jnp.inf, s)
        return s, vals, idxs

    _, vals, idxs = lax.fori_loop(0, PEER_TOPK, body,
                                  (s, jnp.zeros((PEER_TOPK, t), F32), jnp.zeros((PEER_TOPK, t), jnp.int32)))
    return vals, idxs


def _route_kernel(q_ref, keys_ref, i1_ref, i2_ref, g_ref):
    t = q_ref.shape[0]
    key_id = lax.broadcasted_iota(jnp.int32, (PEER_N_KEYS, t), 0)
    r16 = lax.broadcasted_iota(jnp.int32, (PEER_TOPK, t), 0)
    r8 = lax.broadcasted_iota(jnp.int32, (SUBLANES, t), 0)
    pair_id = jnp.concatenate([r16] + [k1 * PEER_TOPK + r8 for k1 in range(1, SUBLANES)]
                              + [(r8 + SUBLANES) * PEER_TOPK], axis=0)
    i1_parts, i2_parts, g_parts = [], [], []
    for h in range(PEER_HEADS):
        tops = []
        for p in range(2):
            c0 = (h * 2 + p) * PEER_N_KEYS
            q_hp = q_ref[:, c0:c0 + PEER_N_KEYS]
            s_t = lax.dot_general(keys_ref[h, p], q_hp, NT, precision=HI, preferred_element_type=F32)
            tops.append(_top16_rows(s_t, key_id, PEER_N_KEYS))
        (v0, x0), (v1, x1) = tops
        cand = jnp.concatenate([v0[0:1, :] + v1]
                               + [v0[k1:k1 + 1, :] + v1[0:SUBLANES, :] for k1 in range(1, SUBLANES)]
                               + [v0[SUBLANES:PEER_TOPK, :] + v1[0:1, :]], axis=0)
        best, pos = _top16_rows(cand, pair_id, PEER_TOPK * PEER_TOPK)
        k1 = pos >> 4
        k2 = pos & (PEER_TOPK - 1)
        a = jnp.zeros(pos.shape, jnp.int32)
        b = jnp.zeros(pos.shape, jnp.int32)
        for k in range(PEER_TOPK):
            a = jnp.where(k1 == k, x0[k:k + 1, :], a)
            b = jnp.where(k2 == k, x1[k:k + 1, :], b)
        e = jnp.exp(best - jnp.max(best, axis=0, keepdims=True))
        g_parts.append(e / jnp.sum(e, axis=0, keepdims=True))
        i1_parts.append(a)
        i2_parts.append(b)
    i1_ref[...] = jnp.concatenate(i1_parts, axis=0).T
    i2_ref[...] = jnp.concatenate(i2_parts, axis=0).T
    g_ref[...] = jnp.concatenate(g_parts, axis=0).T


def _route(q, keys):
    t = q.shape[0]
    tb = min(4 * LANES, t)
    slots = PEER_HEADS * PEER_TOPK
    spec = pl.BlockSpec((tb, slots), lambda i: (i, 0))
    return pl.pallas_call(
        _route_kernel,
        grid=(t // tb,),
        in_specs=[pl.BlockSpec((tb, q.shape[1]), lambda i: (i, 0)),
                  pl.BlockSpec(keys.shape, lambda i: (0, 0, 0, 0))],
        out_specs=[spec, spec, spec],
        out_shape=[jax.ShapeDtypeStruct((t, slots), jnp.int32), jax.ShapeDtypeStruct((t, slots), jnp.int32),
                   jax.ShapeDtypeStruct((t, slots), F32)],
        compiler_params=_cparams(("parallel",), 32),
        name="peer_route",
    )(q, keys)


def _expert_kernel(x_ref, h_ref, i1_ref, i2_ref, g_ref, gf_ref, u_ref, v_ref, o_ref, w_scr, acc_scr, *,
                   final_norm):
    c = pl.program_id(1)
    tb = x_ref.shape[0]
    n_sub = PEER_EXPERT_CHUNK // PEER_N_KEYS
    w_stride = tb + W_PAD

    @pl.when(c == 0)
    def _():
        sub = lax.broadcasted_iota(jnp.int32, (PEER_N_KEYS, PEER_N_KEYS), 0)

        def build(t, carry):
            i1_row = i1_ref[pl.ds(t, 1), :]
            i2_row = i2_ref[pl.ds(t, 1), :]
            g_row = g_ref[pl.ds(t, 1), :]
            p_t = jnp.where(sub == i1_row, g_row, 0.0).astype(BF16)
            q_t = jnp.where(sub == i2_row, 1.0, 0.0).astype(BF16)
            w_t = lax.dot_general(p_t, q_t, NT, preferred_element_type=F32)
            w_scr[pl.ds(t, PEER_N_KEYS, stride=w_stride), :] = w_t
            return carry

        lax.fori_loop(0, tb, build, 0, unroll=32)
        acc_scr[...] = jnp.zeros(acc_scr.shape, F32)

    act = lax.dot_general(h_ref[...], u_ref[...], NT, preferred_element_type=F32)
    gel = 0.5 * act * (1.0 + lax.erf(act * math.sqrt(0.5)))
    parts = []
    for a in range(n_sub):
        w_a = w_scr[pl.ds(pl.multiple_of((c * n_sub + a) * w_stride, SUBLANES), tb), :]
        parts.append((w_a * gel[:, a * PEER_N_KEYS:(a + 1) * PEER_N_KEYS]).astype(BF16))
    wa = jnp.concatenate(parts, axis=1)
    acc_scr[...] += jnp.dot(wa, v_ref[...], preferred_element_type=F32)

    @pl.when(c == pl.num_programs(1) - 1)
    def _():
        y = x_ref[...] + acc_scr[...]
        if final_norm:
            ms = jnp.mean(y * y, axis=-1, keepdims=True)
            y = y * lax.rsqrt(ms + EPS) * gf_ref[...]
        o_ref[...] = y


def _experts(x, h, i1, i2, g, g_final, u, v, layer, *, tb, final_norm):
    t = x.shape[0]
    slots = i1.shape[1]
    row = lambda w: pl.BlockSpec((tb, w), lambda i, c: (i, 0))
    chunk = pl.BlockSpec((None, PEER_EXPERT_CHUNK, D_MODEL), lambda i, c: (layer, c, 0))
    return pl.pallas_call(
        functools.partial(_expert_kernel, final_norm=final_norm),
        grid=(t // tb, PEER_EXPERTS // PEER_EXPERT_CHUNK),
        in_specs=[row(D_MODEL), row(D_MODEL), row(slots), row(slots), row(slots),
                  pl.BlockSpec((1, D_MODEL), lambda i, c: (0, 0)), chunk, chunk],
        out_specs=row(D_MODEL),
        out_shape=jax.ShapeDtypeStruct((t, D_MODEL), F32),
        scratch_shapes=[pltpu.VMEM((PEER_N_KEYS * (tb + W_PAD), PEER_N_KEYS), F32),
                        pltpu.VMEM((tb, D_MODEL), F32)],
        compiler_params=_cparams(("parallel", "arbitrary"), 56),
        name="peer_experts",
    )(x, h, i1, i2, g, g_final.reshape(1, D_MODEL), u, v)


def _layer_tail(x, conv_o, moba_o, diff_o, lw, g_final, *, tm, tb, final_norm):
    x1 = _out_proj(x, conv_o, moba_o, diff_o, lw["wo_c"], lw["wo_m"], lw["wo_d"], tm=min(tm, 256))
    q, h2 = _project(x1, lw["g_ffn"], lw["wq"], lw["layer"], emit_h=True, tm=tm, tn=1024)
    i1, i2, g = _route(q, lw["keys"])
    return _experts(x1, h2, i1, i2, g, g_final, lw["u"], lw["v"], lw["layer"], tb=tb, final_norm=final_norm)


def kernel(x_prompt, x_sample, cache_moba_k, cache_moba_v, cache_diff_k, cache_diff_v, state_conv, page_table,
           g_mix, w_in, conv_w, diff_lambda_q1, diff_lambda_k1, diff_lambda_q2, diff_lambda_k2, diff_subln_g,
           w_out, g_ffn, peer_w_query, peer_sub_keys, peer_expert_u, peer_expert_v, g_final):
    xp = x_prompt.reshape(N_PROMPT, D_MODEL)
    xs = x_sample.reshape(N_SAMPLE, D_MODEL)
    cache_moba_k, cache_moba_v, cache_diff_k, cache_diff_v = (
        _head_major(c) for c in (cache_moba_k, cache_moba_v, cache_diff_k, cache_diff_v))
    prompt_state, sample_state = [], []
    w_in_b, wq_b, u_b, v_b = (w.astype(BF16) for w in (w_in, peer_w_query, peer_expert_u, peer_expert_v))
    for l in range(DEPTH):
        lam_init = 0.8 - 0.6 * math.exp(-0.3 * l)
        final = l == DEPTH - 1
        wo = w_out[l].astype(BF16)
        lw = dict(wo_c=wo[:CONV_CH], wo_m=wo[CONV_CH:CONV_CH + MOBA_HEADS * HEAD_DIM],
                  wo_d=wo[CONV_CH + MOBA_HEADS * HEAD_DIM:], g_ffn=g_ffn[l], wq=wq_b,
                  keys=peer_sub_keys[l], u=u_b, v=v_b, layer=l)
        lams = [a[l].reshape(1, DIFF_QK_DIM) for a in
                (diff_lambda_q1, diff_lambda_k1, diff_lambda_q2, diff_lambda_k2)]

        proj, *new_kv = _project(xp, g_mix[l], w_in_b, l, seq=SEQ, tm=512, tn=STATE_WIDTH)
        conv_o, conv_new = _conv_prompt(proj, conv_w[l])
        moba_o = _moba_prompt(proj)
        diff_o = _diff_prompt(proj, lams, diff_subln_g[l], lam_init)
        xp = _layer_tail(xp, conv_o, moba_o, diff_o, lw, g_final, tm=512, tb=256, final_norm=final)
        prompt_state.append((new_kv, conv_new))

        proj_s = _project(xs, g_mix[l], w_in_b, l, tm=N_SAMPLE, tn=1024)
        proj3 = proj_s.reshape(DEC_BATCH, DEC_SEQ, proj_s.shape[1])
        conv_o, conv_new = _conv_sample(proj_s, conv_w[l], state_conv[l])
        moba_o = _moba_sample(proj3, cache_moba_k, cache_moba_v, page_table, l)
        diff_o = _diff_sample(proj3, cache_diff_k, cache_diff_v, page_table, l, lams, diff_subln_g[l], lam_init)
        xs = _layer_tail(xs, conv_o, moba_o.reshape(N_SAMPLE, -1), diff_o.reshape(N_SAMPLE, -1), lw, g_final,
                         tm=N_SAMPLE, tb=N_SAMPLE, final_norm=final)
        sample_state.append((proj_s, conv_new))

    def sample_states(group):
        cols = lambda c0: jnp.stack([p[:, c0 * LANES:(c0 + MOBA_HEADS) * LANES]
                                     .reshape(DEC_BATCH, DEC_SEQ, MOBA_HEADS, HEAD_DIM) for p, _ in group])
        return cols(COL_MK), cols(COL_MV), cols(COL_DK), cols(COL_DV), jnp.stack([c for _, c in group])

    def prompt_states(group):
        kv = tuple(jnp.transpose(jnp.stack([s[n] for s, _ in group]), (0, 1, 3, 2, 4))
                   for n in range(len(STATE_TILES)))
        return kv + (jnp.stack([c for _, c in group]),)

    y_prompt = xp.reshape(BATCH, SEQ, D_MODEL)
    y_sample = xs.reshape(DEC_BATCH, DEC_SEQ, D_MODEL)
    return (y_prompt, y_sample) + prompt_states(prompt_state) + sample_states(sample_state)
```

```python
import functools
import math

import jax
import jax.numpy as jnp
from jax import lax
from jax.experimental import pallas as pl
from jax.experimental.pallas import tpu as pltpu

D_MODEL = 2048
BATCH = 4
SEQ = 2048
DEPTH = 2
DEC_BATCH = 32
DEC_SEQ = 4
PAST_LEN = 8192
PAGE_SIZE = 128
HEAD_DIM = 128
CONV_CH = 512
CONV_WIDTH = 3
MOBA_HEADS = 6
MOBA_BLOCK = 256
MOBA_TOPK = 3
DIFF_HEADS = 6
DIFF_QK_DIM = 64
PEER_HEADS = 8
PEER_N_KEYS = 128
PEER_EXPERTS = PEER_N_KEYS * PEER_N_KEYS
PEER_TOPK = 16
EPS = 1e-6
NEG = -1e30

N_PAGES = PAST_LEN // PAGE_SIZE
N_SAMPLE = DEC_BATCH * DEC_SEQ
N_PROMPT = BATCH * SEQ
N_PROMPT_BLOCKS = SEQ // MOBA_BLOCK
N_PAST_BLOCKS = PAST_LEN // MOBA_BLOCK

COL_MQ, COL_MK, COL_MV = 12, 18, 24
COL_DQ, COL_DK, COL_DV = 30, 36, 42

LANES = 128
SUBLANES = 8
MIB = 1024 * 1024

PAGES_PER_STEP = 16
N_PAGE_STEPS = N_PAGES // PAGES_PER_STEP
KEYS_PER_STEP = PAGES_PER_STEP * PAGE_SIZE
BLOCKS_PER_STEP = KEYS_PER_STEP // MOBA_BLOCK

PEER_EXPERT_CHUNK = 1024
W_PAD = SUBLANES

F32 = jnp.float32
BF16 = jnp.bfloat16
HI = lax.Precision.HIGHEST
NT = (((1,), (1,)), ((), ()))


def _cparams(sem, vmem_mib):
    return pltpu.CompilerParams(dimension_semantics=sem, vmem_limit_bytes=vmem_mib * MIB)


STATE_WIDTH = MOBA_HEADS * HEAD_DIM
STATE_TILES = tuple(c // MOBA_HEADS for c in (COL_MK, COL_MV, COL_DK, COL_DV))


def _proj_kernel(x_ref, g_ref, w_ref, o_ref, *rest, emit_h, emit_states):
    rest = list(rest)
    hout_ref = rest.pop(0) if emit_h else None
    state_refs = [rest.pop(0) for _ in STATE_TILES] if emit_states else []
    h_scr = rest.pop(0)
    j = pl.program_id(1)

    @pl.when(j == 0)
    def _():
        x = x_ref[...]
        ms = jnp.mean(x * x, axis=-1, keepdims=True)
        hb = (x * lax.rsqrt(ms + EPS) * g_ref[...]).astype(BF16)
        h_scr[...] = hb
        if emit_h:
            hout_ref[...] = hb

    o_ref[...] = jnp.dot(h_scr[...], w_ref[...], preferred_element_type=F32)

    for tile, s_ref in zip(STATE_TILES, state_refs):
        @pl.when(j == tile)
        def _(s_ref=s_ref):
            for h in range(MOBA_HEADS):
                s_ref[h] = o_ref[:, h * HEAD_DIM:(h + 1) * HEAD_DIM]


def _project(x, g, w, layer, *, emit_h=False, seq=None, tm, tn):
    t, d = x.shape
    n = w.shape[2]
    emit_states = seq is not None
    out_shape = [jax.ShapeDtypeStruct((t, n), F32)]
    out_specs = [pl.BlockSpec((tm, tn), lambda i, j: (i, j))]
    if emit_h:
        out_shape.append(jax.ShapeDtypeStruct((t, d), BF16))
        out_specs.append(pl.BlockSpec((tm, d), lambda i, j: (i, 0)))
    if emit_states:
        assert tn == STATE_WIDTH and seq % tm == 0
        per_seq = seq // tm
        for _ in STATE_TILES:
            out_shape.append(jax.ShapeDtypeStruct((t // seq, MOBA_HEADS, seq, HEAD_DIM), F32))
            out_specs.append(pl.BlockSpec((None, MOBA_HEADS, tm, HEAD_DIM),
                                          lambda i, j: (i // per_seq, 0, i % per_seq, 0)))
    res = pl.pallas_call(
        functools.partial(_proj_kernel, emit_h=emit_h, emit_states=emit_states),
        grid=(t // tm, n // tn),
        in_specs=[pl.BlockSpec((tm, d), lambda i, j: (i, 0)),
                  pl.BlockSpec((1, d), lambda i, j: (0, 0)),
                  pl.BlockSpec((None, d, tn), lambda i, j: (layer, 0, j))],
        out_specs=out_specs, out_shape=out_shape,
        scratch_shapes=[pltpu.VMEM((tm, d), BF16)],
        compiler_params=_cparams(("parallel", "arbitrary"), 48),
        name="proj",
    )(x, g.reshape(1, d), w)
    return res if len(res) > 1 else res[0]


def _conv_prompt_kernel(cb_ref, cc_ref, cx_ref, w_ref, y_ref, st_ref, u_scr):
    t = cb_ref.shape[0]
    u = cc_ref[...] * cx_ref[...]
    u_scr[0:SUBLANES, :] = jnp.zeros((SUBLANES, CONV_CH), F32)
    u_scr[SUBLANES:SUBLANES + t, :] = u
    w = w_ref[...]
    y = w[0:1, :] * u_scr[SUBLANES - 2:SUBLANES - 2 + t, :]
    y = y + w[1:2, :] * u_scr[SUBLANES - 1:SUBLANES - 1 + t, :]
    y = y + w[2:3, :] * u
    y_ref[...] = cb_ref[...] * y
    st_ref[...] = u_scr[SUBLANES + t - 2:SUBLANES + t, :]


def _conv_prompt(proj, conv_w):
    return pl.pallas_call(
        _conv_prompt_kernel,
        grid=(BATCH,),
        in_specs=[pl.BlockSpec((SEQ, CONV_CH), lambda b: (b, 0)),
                  pl.BlockSpec((SEQ, CONV_CH), lambda b: (b, 1)),
                  pl.BlockSpec((SEQ, CONV_CH), lambda b: (b, 2)),
                  pl.BlockSpec((CONV_WIDTH, CONV_CH), lambda b: (0, 0))],
        out_specs=[pl.BlockSpec((SEQ, CONV_CH), lambda b: (b, 0)),
                   pl.BlockSpec((None, CONV_WIDTH - 1, CONV_CH), lambda b: (b, 0, 0))],
        out_shape=[jax.ShapeDtypeStruct((N_PROMPT, CONV_CH), F32),
                   jax.ShapeDtypeStruct((BATCH, CONV_WIDTH - 1, CONV_CH), F32)],
        scratch_shapes=[pltpu.VMEM((SEQ + SUBLANES, CONV_CH), F32)],
        compiler_params=_cparams(("parallel",), 48),
        name="conv_prompt",
    )(proj, proj, proj, conv_w)


def _conv_sample_kernel(cb_ref, cc_ref, cx_ref, w_ref, st_in_ref, y_ref, st_ref, u_scr):
    w = w_ref[...]
    for s in range(DEC_BATCH):
        r = s * DEC_SEQ
        u_scr[0:2, :] = st_in_ref[s]
        u_scr[2:2 + DEC_SEQ, :] = cc_ref[r:r + DEC_SEQ, :] * cx_ref[r:r + DEC_SEQ, :]
        y = w[0:1, :] * u_scr[0:DEC_SEQ, :]
        y = y + w[1:2, :] * u_scr[1:1 + DEC_SEQ, :]
        y = y + w[2:3, :] * u_scr[2:2 + DEC_SEQ, :]
        y_ref[r:r + DEC_SEQ, :] = cb_ref[r:r + DEC_SEQ, :] * y
        st_ref[s] = u_scr[DEC_SEQ:DEC_SEQ + 2, :]


def _conv_sample(proj, conv_w, state):
    return pl.pallas_call(
        _conv_sample_kernel,
        grid=(1,),
        in_specs=[pl.BlockSpec((N_SAMPLE, CONV_CH), lambda i: (0, 0)),
                  pl.BlockSpec((N_SAMPLE, CONV_CH), lambda i: (0, 1)),
                  pl.BlockSpec((N_SAMPLE, CONV_CH), lambda i: (0, 2)),
                  pl.BlockSpec((CONV_WIDTH, CONV_CH), lambda i: (0, 0)),
                  pl.BlockSpec((DEC_BATCH, CONV_WIDTH - 1, CONV_CH), lambda i: (0, 0, 0))],
        out_specs=[pl.BlockSpec((N_SAMPLE, CONV_CH), lambda i: (0, 0)),
                   pl.BlockSpec((DEC_BATCH, CONV_WIDTH - 1, CONV_CH), lambda i: (0, 0, 0))],
        out_shape=[jax.ShapeDtypeStruct((N_SAMPLE, CONV_CH), F32),
                   jax.ShapeDtypeStruct((DEC_BATCH, CONV_WIDTH - 1, CONV_CH), F32)],
        scratch_shapes=[pltpu.VMEM((SUBLANES, CONV_CH), F32)],
        name="conv_sample",
    )(proj, proj, proj, conv_w, state)


def _rank_lt(gate, n_blocks, k, axis):
    block_id = lax.broadcasted_iota(jnp.int32, gate.shape, axis)
    rank = jnp.zeros(gate.shape, jnp.int32)
    for n2 in range(n_blocks):
        other = gate[n2:n2 + 1, :] if axis == 0 else gate[:, n2:n2 + 1]
        beats = jnp.where(other > gate, 1, jnp.where(other == gate, jnp.where(block_id > n2, 1, 0), 0))
        rank = rank + beats
    return jnp.where(rank < k, 1, 0)


def _for_past_block_pairs(i, update):
    for a in range(0, N_PROMPT_BLOCKS - 1, 2):
        b = a + 1
        if b < N_PROMPT_BLOCKS - 1:
            pl.when(b < i)(functools.partial(update, (a, b)))
        pl.when(b == i)(functools.partial(update, (a,)))


HEADS_PER_STEP = 2


def _head_cols(hh):
    return slice(hh * HEAD_DIM, (hh + 1) * HEAD_DIM)


def _stage_kv(k_ref, v_ref, kb_scr, vt_scr):
    for hh in range(HEADS_PER_STEP):
        kb_scr[hh] = k_ref[:, _head_cols(hh)].astype(BF16)
        for n in range(N_PROMPT_BLOCKS):
            vt_scr[hh, n] = v_ref[n * MOBA_BLOCK:(n + 1) * MOBA_BLOCK, _head_cols(hh)].T.astype(BF16)


def _prompt_specs(col_q, col_k, col_v):
    tq, width = MOBA_BLOCK, HEADS_PER_STEP * HEAD_DIM
    q_spec = pl.BlockSpec((tq, width), lambda b, g, i: (b * N_PROMPT_BLOCKS + i, col_q // HEADS_PER_STEP + g))
    k_spec = pl.BlockSpec((SEQ, width), lambda b, g, i: (b, col_k // HEADS_PER_STEP + g))
    v_spec = pl.BlockSpec((SEQ, width), lambda b, g, i: (b, col_v // HEADS_PER_STEP + g))
    o_spec = pl.BlockSpec((tq, width), lambda b, g, i: (b * N_PROMPT_BLOCKS + i, g))
    return q_spec, k_spec, v_spec, o_spec


def _moba_prompt_kernel(q_ref, k_ref, v_ref, o_ref, kb_scr, vt_scr, km_scr, m_scr, l_scr, acc_scr):
    i = pl.program_id(2)
    tq = MOBA_BLOCK
    scale = HEAD_DIM ** -0.5

    @pl.when(i == 0)
    def _():
        _stage_kv(k_ref, v_ref, kb_scr, vt_scr)
        for hh in range(HEADS_PER_STEP):
            k = k_ref[:, _head_cols(hh)]
            km_scr[hh] = jnp.sum(k.reshape(N_PROMPT_BLOCKS, MOBA_BLOCK, HEAD_DIM), axis=1) * (1.0 / MOBA_BLOCK)

    own = pl.multiple_of(i * tq, tq)
    key = lax.broadcasted_iota(jnp.int32, (tq, tq), 0)
    qry = lax.broadcasted_iota(jnp.int32, (tq, tq), 1)
    sels, qbs = [], []
    for hh in range(HEADS_PER_STEP):
        q = q_ref[:, _head_cols(hh)]
        gate = lax.dot_general(km_scr[hh], q, NT, precision=HI, preferred_element_type=F32)
        blk = lax.broadcasted_iota(jnp.int32, gate.shape, 0)
        gate = jnp.where(blk < i, gate, NEG)
        sels.append(_rank_lt(gate, N_PROMPT_BLOCKS, MOBA_TOPK, axis=0) * jnp.where(blk < i, 1, 0))
        qb = q.astype(BF16)
        qbs.append(qb)
        s = lax.dot_general(kb_scr[hh, pl.ds(own, tq), :], qb, NT, preferred_element_type=F32) * scale
        s = jnp.where(key <= qry, s, NEG)
        m = jnp.max(s, axis=0, keepdims=True)
        p = jnp.exp(s - m)
        m_scr[hh] = m
        l_scr[hh] = jnp.sum(p, axis=0, keepdims=True)
        acc_scr[hh] = jnp.dot(vt_scr[hh, i], p.astype(BF16), preferred_element_type=F32)

    def past_update(blocks):
        for hh in range(HEADS_PER_STEP):
            scores = []
            for n in blocks:
                sn = lax.dot_general(kb_scr[hh, n * tq:(n + 1) * tq, :], qbs[hh], NT,
                                     preferred_element_type=F32) * scale
                scores.append(jnp.where(sels[hh][n:n + 1, :] > 0, sn, NEG))
            m_old = m_scr[hh]
            m_new = m_old
            for sn in scores:
                m_new = jnp.maximum(m_new, jnp.max(sn, axis=0, keepdims=True))
            alpha = jnp.exp(m_old - m_new)
            l = alpha * l_scr[hh]
            acc = alpha * acc_scr[hh]
            for n, sn in zip(blocks, scores):
                pn = jnp.exp(sn - m_new)
                l = l + jnp.sum(pn, axis=0, keepdims=True)
                acc = acc + jnp.dot(vt_scr[hh, n], pn.astype(BF16), preferred_element_type=F32)
            l_scr[hh] = l
            acc_scr[hh] = acc
            m_scr[hh] = m_new

    _for_past_block_pairs(i, past_update)
    for hh in range(HEADS_PER_STEP):
        o_ref[:, _head_cols(hh)] = (acc_scr[hh] / l_scr[hh]).T


def _moba_prompt(proj):
    tq, hps = MOBA_BLOCK, HEADS_PER_STEP
    q_spec, k_spec, v_spec, o_spec = _prompt_specs(COL_MQ, COL_MK, COL_MV)
    return pl.pallas_call(
        _moba_prompt_kernel,
        grid=(BATCH, MOBA_HEADS // hps, N_PROMPT_BLOCKS),
        in_specs=[q_spec, k_spec, v_spec],
        out_specs=o_spec,
        out_shape=jax.ShapeDtypeStruct((N_PROMPT, MOBA_HEADS * HEAD_DIM), F32),
        scratch_shapes=[pltpu.VMEM((hps, SEQ, HEAD_DIM), BF16),
                        pltpu.VMEM((hps, N_PROMPT_BLOCKS, HEAD_DIM, tq), BF16),
                        pltpu.VMEM((hps, N_PROMPT_BLOCKS, HEAD_DIM), F32),
                        pltpu.VMEM((hps, 1, tq), F32), pltpu.VMEM((hps, 1, tq), F32),
                        pltpu.VMEM((hps, HEAD_DIM, tq), F32)],
        compiler_params=_cparams(("parallel", "parallel", "arbitrary"), 32),
        name="moba_prompt",
    )(proj, proj, proj)


def _diff_lambda(lq1_ref, lk1_ref, lq2_ref, lk2_ref, lam_init):
    a = jnp.sum(lq1_ref[...] * lk1_ref[...], axis=-1, keepdims=True)
    b = jnp.sum(lq2_ref[...] * lk2_ref[...], axis=-1, keepdims=True)
    return jnp.exp(a) - jnp.exp(b) + lam_init


def _diff_finish(o1, o2, lam, g, lam_init, axis=-1):
    o = o1 - lam * o2
    ms = jnp.mean(o * o, axis=axis, keepdims=True)
    return o * lax.rsqrt(ms + EPS) * g * (1.0 - lam_init)


def _split_maps(q):
    lane = lax.broadcasted_iota(jnp.int32, q.shape, 1)
    return jnp.where(lane < DIFF_QK_DIM, q, 0.0), jnp.where(lane >= DIFF_QK_DIM, q, 0.0)


def _diff_prompt_kernel(lq1_ref, lk1_ref, lq2_ref, lk2_ref, g_ref, q_ref, k_ref, v_ref, o_ref,
                        kb_scr, vt_scr, m_scr, l_scr, acc_scr, *, lam_init):
    i = pl.program_id(2)
    tq = MOBA_BLOCK
    scale = DIFF_QK_DIM ** -0.5

    @pl.when(i == 0)
    def _():
        _stage_kv(k_ref, v_ref, kb_scr, vt_scr)

    own = pl.multiple_of(i * tq, tq)
    key = lax.broadcasted_iota(jnp.int32, (tq, tq), 0)
    qry = lax.broadcasted_iota(jnp.int32, (tq, tq), 1)
    chains = []
    for hh in range(HEADS_PER_STEP):
        for c, qc in enumerate(_split_maps(q_ref[:, _head_cols(hh)])):
            chains.append((hh, 2 * hh + c, qc.astype(BF16)))
    for hh, slot, qc in chains:
        s = lax.dot_general(kb_scr[hh, pl.ds(own, tq), :], qc, NT, preferred_element_type=F32) * scale
        s = jnp.where(key <= qry, s, NEG)
        m = jnp.max(s, axis=0, keepdims=True)
        p = jnp.exp(s - m)
        m_scr[slot] = m
        l_scr[slot] = jnp.sum(p, axis=0, keepdims=True)
        acc_scr[slot] = jnp.dot(vt_scr[hh, i], p.astype(BF16), preferred_element_type=F32)

    def past_update(blocks):
        for hh, slot, qc in chains:
            scores = [lax.dot_general(kb_scr[hh, n * tq:(n + 1) * tq, :], qc, NT,
                                      preferred_element_type=F32) * scale for n in blocks]
            m_old = m_scr[slot]
            m_new = m_old
            for sn in scores:
                m_new = jnp.maximum(m_new, jnp.max(sn, axis=0, keepdims=True))
            alpha = jnp.exp(m_old - m_new)
            l = alpha * l_scr[slot]
            acc = alpha * acc_scr[slot]
            for n, sn in zip(blocks, scores):
                pn = jnp.exp(sn - m_new)
                l = l + jnp.sum(pn, axis=0, keepdims=True)
                acc = acc + jnp.dot(vt_scr[hh, n], pn.astype(BF16), preferred_element_type=F32)
            l_scr[slot] = l
            acc_scr[slot] = acc
            m_scr[slot] = m_new

    _for_past_block_pairs(i, past_update)
    lam = _diff_lambda(lq1_ref, lk1_ref, lq2_ref, lk2_ref, lam_init)
    for hh in range(HEADS_PER_STEP):
        y = _diff_finish(acc_scr[2 * hh] / l_scr[2 * hh], acc_scr[2 * hh + 1] / l_scr[2 * hh + 1], lam,
                         g_ref[...], lam_init, axis=0)
        o_ref[:, _head_cols(hh)] = y.T


def _diff_prompt(proj, lams, subln_g, lam_init):
    tq, hps = MOBA_BLOCK, HEADS_PER_STEP
    lam_specs = [pl.BlockSpec((1, DIFF_QK_DIM), lambda b, h, i: (0, 0))] * 4
    q_spec, k_spec, v_spec, o_spec = _prompt_specs(COL_DQ, COL_DK, COL_DV)
    return pl.pallas_call(
        functools.partial(_diff_prompt_kernel, lam_init=lam_init),
        grid=(BATCH, DIFF_HEADS // hps, N_PROMPT_BLOCKS),
        in_specs=lam_specs + [pl.BlockSpec((HEAD_DIM, 1), lambda b, h, i: (0, 0)), q_spec, k_spec, v_spec],
        out_specs=o_spec,
        out_shape=jax.ShapeDtypeStruct((N_PROMPT, DIFF_HEADS * HEAD_DIM), F32),
        scratch_shapes=[pltpu.VMEM((hps, SEQ, HEAD_DIM), BF16),
                        pltpu.VMEM((hps, N_PROMPT_BLOCKS, HEAD_DIM, tq), BF16),
                        pltpu.VMEM((2 * hps, 1, tq), F32), pltpu.VMEM((2 * hps, 1, tq), F32),
                        pltpu.VMEM((2 * hps, HEAD_DIM, tq), F32)],
        compiler_params=_cparams(("parallel", "parallel", "arbitrary"), 32),
        name="diff_prompt",
    )(*lams, subln_g.reshape(HEAD_DIM, 1), proj, proj, proj)


def _page_specs(layer, pinned_phase=None):
    specs = []
    for r in range(PAGES_PER_STEP):
        if pinned_phase is None:
            def imap(b, s, pt, r=r):
                return (layer, pt[b, s * PAGES_PER_STEP + r], 0, 0, 0)
        elif pinned_phase == 1:
            def imap(b, ph, s, pt, r=r):
                st = jnp.where(ph == 0, s, N_PAGE_STEPS - 1)
                return (layer, pt[b, st * PAGES_PER_STEP + r], 0, 0, 0)
        else:
            def imap(b, ph, s, pt, r=r):
                st = jnp.where(ph == 0, 0, s)
                return (layer, pt[b, st * PAGES_PER_STEP + r], 0, 0, 0)
        specs.append(pl.BlockSpec((None, None, MOBA_HEADS, PAGE_SIZE, HEAD_DIM), imap))
    return specs


def _head_major(cache):
    return jnp.transpose(cache, (0, 1, 3, 2, 4))


def _load_pages(refs, h):
    return jnp.concatenate([r[h] for r in refs], axis=0)


def _new_key_mask():
    r = lax.broadcasted_iota(jnp.int32, (SUBLANES, PAGE_SIZE), 0)
    j = lax.broadcasted_iota(jnp.int32, (SUBLANES, PAGE_SIZE), 1)
    return (j <= (r & (DEC_SEQ - 1))) & (j < DEC_SEQ)


def _stage_new_kv(kn_ref, vn_ref, knp_scr, vnp_scr, h):
    knp_scr[...] = jnp.zeros(knp_scr.shape, F32)
    vnp_scr[...] = jnp.zeros(vnp_scr.shape, F32)
    knp_scr[0:DEC_SEQ, :] = kn_ref[:, h * HEAD_DIM:(h + 1) * HEAD_DIM]
    vnp_scr[0:DEC_SEQ, :] = vn_ref[:, h * HEAD_DIM:(h + 1) * HEAD_DIM]


def _diff_sample_kernel(pt_ref, lq1_ref, lk1_ref, lq2_ref, lk2_ref, g_ref, q_ref, kn_ref, vn_ref, *rest,
                        lam_init):
    k_refs = rest[:PAGES_PER_STEP]
    v_refs = rest[PAGES_PER_STEP:2 * PAGES_PER_STEP]
    o_ref = rest[2 * PAGES_PER_STEP]
    q8_scr, knp_scr, vnp_scr, m_scr, l_scr, acc_scr = rest[2 * PAGES_PER_STEP + 1:]
    step = pl.program_id(1)
    scale = DIFF_QK_DIM ** -0.5

    @pl.when(step == 0)
    def _():
        for h in range(DIFF_HEADS):
            q1, q2 = _split_maps(q_ref[:, h * HEAD_DIM:(h + 1) * HEAD_DIM])
            q8_scr[h, 0:DEC_SEQ, :] = q1
            q8_scr[h, DEC_SEQ:2 * DEC_SEQ, :] = q2
        m_scr[...] = jnp.full(m_scr.shape, -jnp.inf, F32)
        l_scr[...] = jnp.zeros(l_scr.shape, F32)
        acc_scr[...] = jnp.zeros(acc_scr.shape, F32)

    def update(h, s, vb):
        m_old = m_scr[h]
        m_new = jnp.maximum(m_old, jnp.max(s, axis=-1, keepdims=True))
        alpha = jnp.exp(m_old - m_new)
        p = jnp.exp(s - m_new)
        l_scr[h] = alpha * l_scr[h] + jnp.sum(p, axis=-1, keepdims=True)
        acc_scr[h] = alpha * acc_scr[h] + jnp.dot(p.astype(BF16), vb, preferred_element_type=F32)
        m_scr[h] = m_new

    for h in range(DIFF_HEADS):
        q8 = q8_scr[h].astype(BF16)
        kb = _load_pages(k_refs, h).astype(BF16)
        vb = _load_pages(v_refs, h).astype(BF16)
        update(h, lax.dot_general(q8, kb, NT, preferred_element_type=F32) * scale, vb)

    @pl.when(step == N_PAGE_STEPS - 1)
    def _():
        lam = _diff_lambda(lq1_ref, lk1_ref, lq2_ref, lk2_ref, lam_init)
        for h in range(DIFF_HEADS):
            q8 = q8_scr[h].astype(BF16)
            _stage_new_kv(kn_ref, vn_ref, knp_scr, vnp_scr, h)
            sn = lax.dot_general(q8, knp_scr[...].astype(BF16), NT, preferred_element_type=F32) * scale
            update(h, jnp.where(_new_key_mask(), sn, NEG), vnp_scr[...].astype(BF16))
            o = acc_scr[h] / l_scr[h]
            o_ref[:, h * HEAD_DIM:(h + 1) * HEAD_DIM] = _diff_finish(
                o[0:DEC_SEQ, :], o[DEC_SEQ:2 * DEC_SEQ, :], lam, g_ref[...], lam_init)


def _tok_spec(c0):
    width = MOBA_HEADS * HEAD_DIM
    return pl.BlockSpec((None, DEC_SEQ, width), lambda b, *_: (b, 0, c0 // MOBA_HEADS))


def _diff_sample(proj3, cache_k, cache_v, page_table, layer, lams, subln_g, lam_init):
    lam_specs = [pl.BlockSpec((1, DIFF_QK_DIM), lambda b, s, pt: (0, 0))] * 4
    grid_spec = pltpu.PrefetchScalarGridSpec(
        num_scalar_prefetch=1,
        grid=(DEC_BATCH, N_PAGE_STEPS),
        in_specs=lam_specs + [pl.BlockSpec((1, HEAD_DIM), lambda b, s, pt: (0, 0)),
                              _tok_spec(COL_DQ), _tok_spec(COL_DK), _tok_spec(COL_DV)]
        + _page_specs(layer) + _page_specs(layer),
        out_specs=pl.BlockSpec((None, DEC_SEQ, DIFF_HEADS * HEAD_DIM), lambda b, s, pt: (b, 0, 0)),
        scratch_shapes=[pltpu.VMEM((DIFF_HEADS, SUBLANES, HEAD_DIM), F32),
                        pltpu.VMEM((PAGE_SIZE, HEAD_DIM), F32), pltpu.VMEM((PAGE_SIZE, HEAD_DIM), F32),
                        pltpu.VMEM((DIFF_HEADS, SUBLANES, 1), F32), pltpu.VMEM((DIFF_HEADS, SUBLANES, 1), F32),
                        pltpu.VMEM((DIFF_HEADS, SUBLANES, HEAD_DIM), F32)])
    return pl.pallas_call(
        functools.partial(_diff_sample_kernel, lam_init=lam_init),
        grid_spec=grid_spec,
        out_shape=jax.ShapeDtypeStruct((DEC_BATCH, DEC_SEQ, DIFF_HEADS * HEAD_DIM), F32),
        compiler_params=_cparams(("parallel", "arbitrary"), 56),
        name="diff_sample",
    )(page_table, *lams, subln_g.reshape(1, HEAD_DIM), proj3, proj3, proj3,
      *([cache_k] * PAGES_PER_STEP), *([cache_v] * PAGES_PER_STEP))


def _moba_sample_kernel(pt_ref, q_ref, kn_ref, vn_ref, *rest):
    k_refs = rest[:PAGES_PER_STEP]
    v_refs = rest[PAGES_PER_STEP:2 * PAGES_PER_STEP]
    o_ref = rest[2 * PAGES_PER_STEP]
    q8_scr, knp_scr, vnp_scr, s_scr, km_scr, p_scr, l_scr, acc_scr = rest[2 * PAGES_PER_STEP + 1:]
    phase = pl.program_id(1)
    step = pl.program_id(2)
    scale = HEAD_DIM ** -0.5

    @pl.when((phase == 0) & (step == 0))
    def _():
        q8_scr[...] = jnp.zeros(q8_scr.shape, F32)
        for h in range(MOBA_HEADS):
            q8_scr[h, 0:DEC_SEQ, :] = q_ref[:, h * HEAD_DIM:(h + 1) * HEAD_DIM]

    @pl.when(phase == 0)
    def _():
        for h in range(MOBA_HEADS):
            k = _load_pages(k_refs, h)
            km_scr[h, step] = (jnp.sum(k.reshape(BLOCKS_PER_STEP, MOBA_BLOCK, HEAD_DIM), axis=1)
                               * (1.0 / MOBA_BLOCK))
            s_scr[h, step] = lax.dot_general(q8_scr[h].astype(BF16), k.astype(BF16), NT,
                                             preferred_element_type=F32) * scale

    @pl.when((phase == 1) & (step == 0))
    def _():
        for h in range(MOBA_HEADS):
            q8 = q8_scr[h]
            km = km_scr[h].reshape(N_PAST_BLOCKS, HEAD_DIM)
            gate = lax.dot_general(q8, km, NT, precision=HI, preferred_element_type=F32)
            sel = _rank_lt(gate, N_PAST_BLOCKS, MOBA_TOPK, axis=1)
            _stage_new_kv(kn_ref, vn_ref, knp_scr, vnp_scr, h)
            sn = lax.dot_general(q8.astype(BF16), knp_scr[...].astype(BF16), NT,
                                 preferred_element_type=F32) * scale
            sn = jnp.where(_new_key_mask(), sn, NEG)
            m = jnp.max(sn, axis=-1, keepdims=True)

            def block(n, h=h, sel=sel):
                st, jb = divmod(n, BLOCKS_PER_STEP)
                return jnp.where(sel[:, n:n + 1] > 0,
                                 s_scr[h, st, :, jb * MOBA_BLOCK:(jb + 1) * MOBA_BLOCK], NEG)

            for n in range(N_PAST_BLOCKS):
                m = jnp.maximum(m, jnp.max(block(n), axis=-1, keepdims=True))
            pn = jnp.exp(sn - m)
            l = jnp.sum(pn, axis=-1, keepdims=True)
            for n in range(N_PAST_BLOCKS):
                st, jb = divmod(n, BLOCKS_PER_STEP)
                pb = jnp.exp(block(n) - m)
                l = l + jnp.sum(pb, axis=-1, keepdims=True)
                p_scr[h, st, :, jb * MOBA_BLOCK:(jb + 1) * MOBA_BLOCK] = pb
            l_scr[h] = l
            acc_scr[h] = jnp.dot(pn.astype(BF16), vnp_scr[...].astype(BF16), preferred_element_type=F32)

    @pl.when(phase == 1)
    def _():
        for h in range(MOBA_HEADS):
            vb = _load_pages(v_refs, h).astype(BF16)
            acc_scr[h] += jnp.dot(p_scr[h, step].astype(BF16), vb, preferred_element_type=F32)

    @pl.when((phase == 1) & (step == N_PAGE_STEPS - 1))
    def _():
        for h in range(MOBA_HEADS):
            o = acc_scr[h] / l_scr[h]
            o_ref[:, h * HEAD_DIM:(h + 1) * HEAD_DIM] = o[0:DEC_SEQ, :]


def _moba_sample(proj3, cache_k, cache_v, page_table, layer):
    grid_spec = pltpu.PrefetchScalarGridSpec(
        num_scalar_prefetch=1,
        grid=(DEC_BATCH, 2, N_PAGE_STEPS),
        in_specs=[_tok_spec(COL_MQ), _tok_spec(COL_MK), _tok_spec(COL_MV)]
        + _page_specs(layer, pinned_phase=1) + _page_specs(layer, pinned_phase=0),
        out_specs=pl.BlockSpec((None, DEC_SEQ, MOBA_HEADS * HEAD_DIM), lambda b, ph, s, pt: (b, 0, 0)),
        scratch_shapes=[pltpu.VMEM((MOBA_HEADS, SUBLANES, HEAD_DIM), F32),
                        pltpu.VMEM((PAGE_SIZE, HEAD_DIM), F32), pltpu.VMEM((PAGE_SIZE, HEAD_DIM), F32),
                        pltpu.VMEM((MOBA_HEADS, N_PAGE_STEPS, SUBLANES, KEYS_PER_STEP), F32),
                        pltpu.VMEM((MOBA_HEADS, N_PAGE_STEPS, BLOCKS_PER_STEP, HEAD_DIM), F32),
                        pltpu.VMEM((MOBA_HEADS, N_PAGE_STEPS, SUBLANES, KEYS_PER_STEP), F32),
                        pltpu.VMEM((MOBA_HEADS, SUBLANES, 1), F32),
                        pltpu.VMEM((MOBA_HEADS, SUBLANES, HEAD_DIM), F32)])
    return pl.pallas_call(
        _moba_sample_kernel,
        grid_spec=grid_spec,
        out_shape=jax.ShapeDtypeStruct((DEC_BATCH, DEC_SEQ, MOBA_HEADS * HEAD_DIM), F32),
        compiler_params=_cparams(("parallel", "arbitrary", "arbitrary"), 56),
        name="moba_sample",
    )(page_table, proj3, proj3, proj3, *([cache_k] * PAGES_PER_STEP), *([cache_v] * PAGES_PER_STEP))


def _out_proj_kernel(x_ref, c_ref, m_ref, d_ref, wc_ref, wm_ref, wd_ref, o_ref):
    acc = jnp.dot(c_ref[...].astype(BF16), wc_ref[...], preferred_element_type=F32)
    acc += jnp.dot(m_ref[...].astype(BF16), wm_ref[...], preferred_element_type=F32)
    acc += jnp.dot(d_ref[...].astype(BF16), wd_ref[...], preferred_element_type=F32)
    o_ref[...] = x_ref[...] + acc


def _out_proj(x, conv_o, moba_o, diff_o, wc, wm, wd, tm):
    t = x.shape[0]
    row = lambda w: pl.BlockSpec((tm, w), lambda i: (i, 0))
    full = lambda a: pl.BlockSpec(a.shape, lambda i: (0, 0))
    return pl.pallas_call(
        _out_proj_kernel,
        grid=(t // tm,),
        in_specs=[row(D_MODEL), row(conv_o.shape[1]), row(moba_o.shape[1]), row(diff_o.shape[1]),
                  full(wc), full(wm), full(wd)],
        out_specs=row(D_MODEL),
        out_shape=jax.ShapeDtypeStruct((t, D_MODEL), F32),
        compiler_params=_cparams(("parallel",), 48),
        name="out_proj",
    )(x, conv_o, moba_o, diff_o, wc, wm, wd)


def _top16_rows(s, row_id, id_bound):
    t = s.shape[1]
    slot = lax.broadcasted_iota(jnp.int32, (PEER_TOPK, t), 0)

    def body(k, carry):
        s, vals, idxs = carry
        m = jnp.max(s, axis=0, keepdims=True)
        idx = jnp.min(jnp.where(s == m, row_id, id_bound), axis=0, keepdims=True)
        vals = jnp.where(slot == k, m, vals)
        idxs = jnp.where(slot == k, idx, idxs)
        s = jnp.where(row_id == idx, -jnp.inf, s)
        return s, vals, idxs

    _, vals, idxs = lax.fori_loop(0, PEER_TOPK, body,
                                  (s, jnp.zeros((PEER_TOPK, t), F32), jnp.zeros((PEER_TOPK, t), jnp.int32)))
    return vals, idxs


def _route_kernel(q_ref, keys_ref, i1_ref, i2_ref, g_ref):
    t = q_ref.shape[0]
    key_id = lax.broadcasted_iota(jnp.int32, (PEER_N_KEYS, t), 0)
    r16 = lax.broadcasted_iota(jnp.int32, (PEER_TOPK, t), 0)
    r8 = lax.broadcasted_iota(jnp.int32, (SUBLANES, t), 0)
    pair_id = jnp.concatenate([r16] + [k1 * PEER_TOPK + r8 for k1 in range(1, SUBLANES)]
                              + [(r8 + SUBLANES) * PEER_TOPK], axis=0)
    i1_parts, i2_parts, g_parts = [], [], []
    for h in range(PEER_HEADS):
        tops = []
        for p in range(2):
            c0 = (h * 2 + p) * PEER_N_KEYS
            q_hp = q_ref[:, c0:c0 + PEER_N_KEYS]
            s_t = lax.dot_general(keys_ref[h, p], q_hp, NT, precision=HI, preferred_element_type=F32)
            tops.append(_top16_rows(s_t, key_id, PEER_N_KEYS))
        (v0, x0), (v1, x1) = tops
        cand = jnp.concatenate([v0[0:1, :] + v1]
                               + [v0[k1:k1 + 1, :] + v1[0:SUBLANES, :] for k1 in range(1, SUBLANES)]
                               + [v0[SUBLANES:PEER_TOPK, :] + v1[0:1, :]], axis=0)
        best, pos = _top16_rows(cand, pair_id, PEER_TOPK * PEER_TOPK)
        k1 = pos >> 4
        k2 = pos & (PEER_TOPK - 1)
        a = jnp.zeros(pos.shape, jnp.int32)
        b = jnp.zeros(pos.shape, jnp.int32)
        for k in range(PEER_TOPK):
            a = jnp.where(k1 == k, x0[k:k + 1, :], a)
            b = jnp.where(k2 == k, x1[k:k + 1, :], b)
        e = jnp.exp(best - jnp.max(best, axis=0, keepdims=True))
        g_parts.append(e / jnp.sum(e, axis=0, keepdims=True))
        i1_parts.append(a)
        i2_parts.append(b)
    i1_ref[...] = jnp.concatenate(i1_parts, axis=0).T
    i2_ref[...] = jnp.concatenate(i2_parts, axis=0).T
    g_ref[...] = jnp.concatenate(g_parts, axis=0).T


def _route(q, keys):
    t = q.shape[0]
    tb = min(4 * LANES, t)
    slots = PEER_HEADS * PEER_TOPK
    spec = pl.BlockSpec((tb, slots), lambda i: (i, 0))
    return pl.pallas_call(
        _route_kernel,
        grid=(t // tb,),
        in_specs=[pl.BlockSpec((tb, q.shape[1]), lambda i: (i, 0)),
                  pl.BlockSpec(keys.shape, lambda i: (0, 0, 0, 0))],
        out_specs=[spec, spec, spec],
        out_shape=[jax.ShapeDtypeStruct((t, slots), jnp.int32), jax.ShapeDtypeStruct((t, slots), jnp.int32),
                   jax.ShapeDtypeStruct((t, slots), F32)],
        compiler_params=_cparams(("parallel",), 32),
        name="peer_route",
    )(q, keys)


def _expert_kernel(x_ref, h_ref, i1_ref, i2_ref, g_ref, gf_ref, u_ref, v_ref, o_ref, w_scr, acc_scr, *,
                   final_norm):
    c = pl.program_id(1)
    tb = x_ref.shape[0]
    n_sub = PEER_EXPERT_CHUNK // PEER_N_KEYS
    w_stride = tb + W_PAD

    @pl.when(c == 0)
    def _():
        sub = lax.broadcasted_iota(jnp.int32, (PEER_N_KEYS, PEER_N_KEYS), 0)

        def build(t, carry):
            i1_row = i1_ref[pl.ds(t, 1), :]
            i2_row = i2_ref[pl.ds(t, 1), :]
            g_row = g_ref[pl.ds(t, 1), :]
            p_t = jnp.where(sub == i1_row, g_row, 0.0).astype(BF16)
            q_t = jnp.where(sub == i2_row, 1.0, 0.0).astype(BF16)
            w_t = lax.dot_general(p_t, q_t, NT, preferred_element_type=F32)
            w_scr[pl.ds(t, PEER_N_KEYS, stride=w_stride), :] = w_t
            return carry

        lax.fori_loop(0, tb, build, 0, unroll=32)
        acc_scr[...] = jnp.zeros(acc_scr.shape, F32)

    act = lax.dot_general(h_ref[...], u_ref[...], NT, preferred_element_type=F32)
    gel = 0.5 * act * (1.0 + lax.erf(act * math.sqrt(0.5)))
    parts = []
    for a in range(n_sub):
        w_a = w_scr[pl.ds(pl.multiple_of((c * n_sub + a) * w_stride, SUBLANES), tb), :]
        parts.append((w_a * gel[:, a * PEER_N_KEYS:(a + 1) * PEER_N_KEYS]).astype(BF16))
    wa = jnp.concatenate(parts, axis=1)
    acc_scr[...] += jnp.dot(wa, v_ref[...], preferred_element_type=F32)

    @pl.when(c == pl.num_programs(1) - 1)
    def _():
        y = x_ref[...] + acc_scr[...]
        if final_norm:
            ms = jnp.mean(y * y, axis=-1, keepdims=True)
            y = y * lax.rsqrt(ms + EPS) * gf_ref[...]
        o_ref[...] = y


def _experts(x, h, i1, i2, g, g_final, u, v, layer, *, tb, final_norm):
    t = x.shape[0]
    slots = i1.shape[1]
    row = lambda w: pl.BlockSpec((tb, w), lambda i, c: (i, 0))
    chunk = pl.BlockSpec((None, PEER_EXPERT_CHUNK, D_MODEL), lambda i, c: (layer, c, 0))
    return pl.pallas_call(
        functools.partial(_expert_kernel, final_norm=final_norm),
        grid=(t // tb, PEER_EXPERTS // PEER_EXPERT_CHUNK),
        in_specs=[row(D_MODEL), row(D_MODEL), row(slots), row(slots), row(slots),
                  pl.BlockSpec((1, D_MODEL), lambda i, c: (0, 0)), chunk, chunk],
        out_specs=row(D_MODEL),
        out_shape=jax.ShapeDtypeStruct((t, D_MODEL), F32),
        scratch_shapes=[pltpu.VMEM((PEER_N_KEYS * (tb + W_PAD), PEER_N_KEYS), F32),
                        pltpu.VMEM((tb, D_MODEL), F32)],
        compiler_params=_cparams(("parallel", "arbitrary"), 56),
        name="peer_experts",
    )(x, h, i1, i2, g, g_final.reshape(1, D_MODEL), u, v)


def _layer_tail(x, conv_o, moba_o, diff_o, lw, g_final, *, tm, tb, final_norm):
    x1 = _out_proj(x, conv_o, moba_o, diff_o, lw["wo_c"], lw["wo_m"], lw["wo_d"], tm=min(tm, 256))
    q, h2 = _project(x1, lw["g_ffn"], lw["wq"], lw["layer"], emit_h=True, tm=tm, tn=1024)
    i1, i2, g = _route(q, lw["keys"])
    return _experts(x1, h2, i1, i2, g, g_final, lw["u"], lw["v"], 0, tb=tb, final_norm=final_norm)


def kernel(x_prompt, x_sample, cache_moba_k, cache_moba_v, cache_diff_k, cache_diff_v, state_conv, page_table,
           g_mix, w_in, conv_w, diff_lambda_q1, diff_lambda_k1, diff_lambda_q2, diff_lambda_k2, diff_subln_g,
           w_out, g_ffn, peer_w_query, peer_sub_keys, peer_expert_u, peer_expert_v, g_final):
    xp = x_prompt.reshape(N_PROMPT, D_MODEL)
    xs = x_sample.reshape(N_SAMPLE, D_MODEL)
    cache_moba_k, cache_moba_v, cache_diff_k, cache_diff_v = (
        _head_major(c) for c in (cache_moba_k, cache_moba_v, cache_diff_k, cache_diff_v))
    prompt_state, sample_state = [], []
    w_in_b, wq_b = w_in.astype(BF16), peer_w_query.astype(BF16)
    for l in range(DEPTH):
        lam_init = 0.8 - 0.6 * math.exp(-0.3 * l)
        final = l == DEPTH - 1
        wo = w_out[l].astype(BF16)
        u_l, v_l = (lax.optimization_barrier(w[l:l + 1]).astype(BF16) for w in (peer_expert_u, peer_expert_v))
        lw = dict(wo_c=wo[:CONV_CH], wo_m=wo[CONV_CH:CONV_CH + MOBA_HEADS * HEAD_DIM],
                  wo_d=wo[CONV_CH + MOBA_HEADS * HEAD_DIM:], g_ffn=g_ffn[l], wq=wq_b,
                  keys=peer_sub_keys[l], u=u_l, v=v_l, layer=l)
        lams = [a[l].reshape(1, DIFF_QK_DIM) for a in
                (diff_lambda_q1, diff_lambda_k1, diff_lambda_q2, diff_lambda_k2)]

        proj, *new_kv = _project(xp, g_mix[l], w_in_b, l, seq=SEQ, tm=512, tn=STATE_WIDTH)
        conv_o, conv_new = _conv_prompt(proj, conv_w[l])
        moba_o = _moba_prompt(proj)
        diff_o = _diff_prompt(proj, lams, diff_subln_g[l], lam_init)
        xp = _layer_tail(xp, conv_o, moba_o, diff_o, lw, g_final, tm=512, tb=256, final_norm=final)
        prompt_state.append((new_kv, conv_new))

        proj_s = _project(xs, g_mix[l], w_in_b, l, tm=N_SAMPLE, tn=1024)
        proj3 = proj_s.reshape(DEC_BATCH, DEC_SEQ, proj_s.shape[1])
        conv_o, conv_new = _conv_sample(proj_s, conv_w[l], state_conv[l])
        moba_o = _moba_sample(proj3, cache_moba_k, cache_moba_v, page_table, l)
        diff_o = _diff_sample(proj3, cache_diff_k, cache_diff_v, page_table, l, lams, diff_subln_g[l], lam_init)
        xs = _layer_tail(xs, conv_o, moba_o.reshape(N_SAMPLE, -1), diff_o.reshape(N_SAMPLE, -1), lw, g_final,
                         tm=N_SAMPLE, tb=N_SAMPLE, final_norm=final)
        sample_state.append((proj_s, conv_new))

    def sample_states(group):
        cols = lambda c0: jnp.stack([p[:, c0 * LANES:(c0 + MOBA_HEADS) * LANES]
                                     .reshape(DEC_BATCH, DEC_SEQ, MOBA_HEADS, HEAD_DIM) for p, _ in group])
        return cols(COL_MK), cols(COL_MV), cols(COL_DK), cols(COL_DV), jnp.stack([c for _, c in group])

    def prompt_states(group):
        kv = tuple(jnp.transpose(jnp.stack([s[n] for s, _ in group]), (0, 1, 3, 2, 4))
                   for n in range(len(STATE_TILES)))
        return kv + (jnp.stack([c for _, c in group]),)

    y_prompt = xp.reshape(BATCH, SEQ, D_MODEL)
    y_sample = xs.reshape(DEC_BATCH, DEC_SEQ, D_MODEL)
    return (y_prompt, y_sample) + prompt_states(prompt_state) + sample_states(sample_state)
```

```python
import functools
import math

import jax
import jax.numpy as jnp
from jax import lax
from jax.experimental import pallas as pl
from jax.experimental.pallas import tpu as pltpu

D_MODEL = 2048
BATCH = 4
SEQ = 2048
DEPTH = 2
DEC_BATCH = 32
DEC_SEQ = 4
PAST_LEN = 8192
PAGE_SIZE = 128
HEAD_DIM = 128
CONV_CH = 512
CONV_WIDTH = 3
MOBA_HEADS = 6
MOBA_BLOCK = 256
MOBA_TOPK = 3
DIFF_HEADS = 6
DIFF_QK_DIM = 64
PEER_HEADS = 8
PEER_N_KEYS = 128
PEER_EXPERTS = PEER_N_KEYS * PEER_N_KEYS
PEER_TOPK = 16
EPS = 1e-6
NEG = -1e30

N_PAGES = PAST_LEN // PAGE_SIZE
N_SAMPLE = DEC_BATCH * DEC_SEQ
N_PROMPT = BATCH * SEQ
N_PROMPT_BLOCKS = SEQ // MOBA_BLOCK
N_PAST_BLOCKS = PAST_LEN // MOBA_BLOCK

COL_MQ, COL_MK, COL_MV = 12, 18, 24
COL_DQ, COL_DK, COL_DV = 30, 36, 42

LANES = 128
SUBLANES = 8
MIB = 1024 * 1024

PAGES_PER_STEP = 16
N_PAGE_STEPS = N_PAGES // PAGES_PER_STEP
KEYS_PER_STEP = PAGES_PER_STEP * PAGE_SIZE
BLOCKS_PER_STEP = KEYS_PER_STEP // MOBA_BLOCK

PEER_EXPERT_CHUNK = 1024
W_PAD = SUBLANES

F32 = jnp.float32
BF16 = jnp.bfloat16
HI = lax.Precision.HIGHEST
NT = (((1,), (1,)), ((), ()))


def _cparams(sem, vmem_mib):
    return pltpu.CompilerParams(dimension_semantics=sem, vmem_limit_bytes=vmem_mib * MIB)


STATE_WIDTH = MOBA_HEADS * HEAD_DIM
STATE_TILES = tuple(c // MOBA_HEADS for c in (COL_MK, COL_MV, COL_DK, COL_DV))


def _proj_kernel(x_ref, g_ref, w_ref, o_ref, *rest, emit_h, emit_states):
    rest = list(rest)
    hout_ref = rest.pop(0) if emit_h else None
    state_refs = [rest.pop(0) for _ in STATE_TILES] if emit_states else []
    h_scr = rest.pop(0)
    j = pl.program_id(1)

    @pl.when(j == 0)
    def _():
        x = x_ref[...]
        ms = jnp.mean(x * x, axis=-1, keepdims=True)
        hb = (x * lax.rsqrt(ms + EPS) * g_ref[...]).astype(BF16)
        h_scr[...] = hb
        if emit_h:
            hout_ref[...] = hb

    o_ref[...] = jnp.dot(h_scr[...], w_ref[...], preferred_element_type=F32)

    for tile, s_ref in zip(STATE_TILES, state_refs):
        @pl.when(j == tile)
        def _(s_ref=s_ref):
            for h in range(MOBA_HEADS):
                s_ref[h] = o_ref[:, h * HEAD_DIM:(h + 1) * HEAD_DIM]


def _project(x, g, w, layer, *, emit_h=False, seq=None, tm, tn):
    t, d = x.shape
    n = w.shape[2]
    emit_states = seq is not None
    out_shape = [jax.ShapeDtypeStruct((t, n), F32)]
    out_specs = [pl.BlockSpec((tm, tn), lambda i, j: (i, j))]
    if emit_h:
        out_shape.append(jax.ShapeDtypeStruct((t, d), BF16))
        out_specs.append(pl.BlockSpec((tm, d), lambda i, j: (i, 0)))
    if emit_states:
        assert tn == STATE_WIDTH and seq % tm == 0
        per_seq = seq // tm
        for _ in STATE_TILES:
            out_shape.append(jax.ShapeDtypeStruct((t // seq, MOBA_HEADS, seq, HEAD_DIM), F32))
            out_specs.append(pl.BlockSpec((None, MOBA_HEADS, tm, HEAD_DIM),
                                          lambda i, j: (i // per_seq, 0, i % per_seq, 0)))
    res = pl.pallas_call(
        functools.partial(_proj_kernel, emit_h=emit_h, emit_states=emit_states),
        grid=(t // tm, n // tn),
        in_specs=[pl.BlockSpec((tm, d), lambda i, j: (i, 0)),
                  pl.BlockSpec((1, d), lambda i, j: (0, 0)),
                  pl.BlockSpec((None, d, tn), lambda i, j: (layer, 0, j))],
        out_specs=out_specs, out_shape=out_shape,
        scratch_shapes=[pltpu.VMEM((tm, d), BF16)],
        compiler_params=_cparams(("parallel", "arbitrary"), 48),
        name="proj",
    )(x, g.reshape(1, d), w)
    return res if len(res) > 1 else res[0]


def _conv_prompt_kernel(cb_ref, cc_ref, cx_ref, w_ref, y_ref, st_ref, u_scr):
    t = cb_ref.shape[0]
    u = cc_ref[...] * cx_ref[...]
    u_scr[0:SUBLANES, :] = jnp.zeros((SUBLANES, CONV_CH), F32)
    u_scr[SUBLANES:SUBLANES + t, :] = u
    w = w_ref[...]
    y = w[0:1, :] * u_scr[SUBLANES - 2:SUBLANES - 2 + t, :]
    y = y + w[1:2, :] * u_scr[SUBLANES - 1:SUBLANES - 1 + t, :]
    y = y + w[2:3, :] * u
    y_ref[...] = cb_ref[...] * y
    st_ref[...] = u_scr[SUBLANES + t - 2:SUBLANES + t, :]


def _conv_prompt(proj, conv_w):
    return pl.pallas_call(
        _conv_prompt_kernel,
        grid=(BATCH,),
        in_specs=[pl.BlockSpec((SEQ, CONV_CH), lambda b: (b, 0)),
                  pl.BlockSpec((SEQ, CONV_CH), lambda b: (b, 1)),
                  pl.BlockSpec((SEQ, CONV_CH), lambda b: (b, 2)),
                  pl.BlockSpec((CONV_WIDTH, CONV_CH), lambda b: (0, 0))],
        out_specs=[pl.BlockSpec((SEQ, CONV_CH), lambda b: (b, 0)),
                   pl.BlockSpec((None, CONV_WIDTH - 1, CONV_CH), lambda b: (b, 0, 0))],
        out_shape=[jax.ShapeDtypeStruct((N_PROMPT, CONV_CH), F32),
                   jax.ShapeDtypeStruct((BATCH, CONV_WIDTH - 1, CONV_CH), F32)],
        scratch_shapes=[pltpu.VMEM((SEQ + SUBLANES, CONV_CH), F32)],
        compiler_params=_cparams(("parallel",), 48),
        name="conv_prompt",
    )(proj, proj, proj, conv_w)


def _conv_sample_kernel(cb_ref, cc_ref, cx_ref, w_ref, st_in_ref, y_ref, st_ref, u_scr):
    w = w_ref[...]
    for s in range(DEC_BATCH):
        r = s * DEC_SEQ
        u_scr[0:2, :] = st_in_ref[s]
        u_scr[2:2 + DEC_SEQ, :] = cc_ref[r:r + DEC_SEQ, :] * cx_ref[r:r + DEC_SEQ, :]
        y = w[0:1, :] * u_scr[0:DEC_SEQ, :]
        y = y + w[1:2, :] * u_scr[1:1 + DEC_SEQ, :]
        y = y + w[2:3, :] * u_scr[2:2 + DEC_SEQ, :]
        y_ref[r:r + DEC_SEQ, :] = cb_ref[r:r + DEC_SEQ, :] * y
        st_ref[s] = u_scr[DEC_SEQ:DEC_SEQ + 2, :]


def _conv_sample(proj, conv_w, state):
    return pl.pallas_call(
        _conv_sample_kernel,
        grid=(1,),
        in_specs=[pl.BlockSpec((N_SAMPLE, CONV_CH), lambda i: (0, 0)),
                  pl.BlockSpec((N_SAMPLE, CONV_CH), lambda i: (0, 1)),
                  pl.BlockSpec((N_SAMPLE, CONV_CH), lambda i: (0, 2)),
                  pl.BlockSpec((CONV_WIDTH, CONV_CH), lambda i: (0, 0)),
                  pl.BlockSpec((DEC_BATCH, CONV_WIDTH - 1, CONV_CH), lambda i: (0, 0, 0))],
        out_specs=[pl.BlockSpec((N_SAMPLE, CONV_CH), lambda i: (0, 0)),
                   pl.BlockSpec((DEC_BATCH, CONV_WIDTH - 1, CONV_CH), lambda i: (0, 0, 0))],
        out_shape=[jax.ShapeDtypeStruct((N_SAMPLE, CONV_CH), F32),
                   jax.ShapeDtypeStruct((DEC_BATCH, CONV_WIDTH - 1, CONV_CH), F32)],
        scratch_shapes=[pltpu.VMEM((SUBLANES, CONV_CH), F32)],
        name="conv_sample",
    )(proj, proj, proj, conv_w, state)


def _rank_lt(gate, n_blocks, k, axis):
    block_id = lax.broadcasted_iota(jnp.int32, gate.shape, axis)
    rank = jnp.zeros(gate.shape, jnp.int32)
    for n2 in range(n_blocks):
        other = gate[n2:n2 + 1, :] if axis == 0 else gate[:, n2:n2 + 1]
        beats = jnp.where(other > gate, 1, jnp.where(other == gate, jnp.where(block_id > n2, 1, 0), 0))
        rank = rank + beats
    return jnp.where(rank < k, 1, 0)


def _for_past_block_pairs(i, update):
    for a in range(0, N_PROMPT_BLOCKS - 1, 2):
        b = a + 1
        if b < N_PROMPT_BLOCKS - 1:
            pl.when(b < i)(functools.partial(update, (a, b)))
        pl.when(b == i)(functools.partial(update, (a,)))


HEADS_PER_STEP = 3


def _head_cols(hh):
    return slice(hh * HEAD_DIM, (hh + 1) * HEAD_DIM)


def _stage_kv(k_ref, v_ref, kb_scr, vt_scr):
    for hh in range(HEADS_PER_STEP):
        kb_scr[hh] = k_ref[:, _head_cols(hh)].astype(BF16)
        for n in range(N_PROMPT_BLOCKS):
            vt_scr[hh, n] = v_ref[n * MOBA_BLOCK:(n + 1) * MOBA_BLOCK, _head_cols(hh)].T.astype(BF16)


def _prompt_specs(col_q, col_k, col_v):
    tq, width = MOBA_BLOCK, HEADS_PER_STEP * HEAD_DIM
    q_spec = pl.BlockSpec((tq, width), lambda b, g, i: (b * N_PROMPT_BLOCKS + i, col_q // HEADS_PER_STEP + g))
    k_spec = pl.BlockSpec((SEQ, width), lambda b, g, i: (b, col_k // HEADS_PER_STEP + g))
    v_spec = pl.BlockSpec((SEQ, width), lambda b, g, i: (b, col_v // HEADS_PER_STEP + g))
    o_spec = pl.BlockSpec((tq, width), lambda b, g, i: (b * N_PROMPT_BLOCKS + i, g))
    return q_spec, k_spec, v_spec, o_spec


def _moba_prompt_kernel(q_ref, k_ref, v_ref, o_ref, kb_scr, vt_scr, km_scr, m_scr, l_scr, acc_scr):
    i = pl.program_id(2)
    tq = MOBA_BLOCK
    scale = HEAD_DIM ** -0.5

    @pl.when(i == 0)
    def _():
        _stage_kv(k_ref, v_ref, kb_scr, vt_scr)
        for hh in range(HEADS_PER_STEP):
            k = k_ref[:, _head_cols(hh)]
            km_scr[hh] = jnp.sum(k.reshape(N_PROMPT_BLOCKS, MOBA_BLOCK, HEAD_DIM), axis=1) * (1.0 / MOBA_BLOCK)

    own = pl.multiple_of(i * tq, tq)
    key = lax.broadcasted_iota(jnp.int32, (tq, tq), 0)
    qry = lax.broadcasted_iota(jnp.int32, (tq, tq), 1)
    sels, qbs = [], []
    for hh in range(HEADS_PER_STEP):
        q = q_ref[:, _head_cols(hh)]
        gate = lax.dot_general(km_scr[hh], q, NT, precision=HI, preferred_element_type=F32)
        blk = lax.broadcasted_iota(jnp.int32, gate.shape, 0)
        gate = jnp.where(blk < i, gate, NEG)
        sels.append(_rank_lt(gate, N_PROMPT_BLOCKS, MOBA_TOPK, axis=0) * jnp.where(blk < i, 1, 0))
        qb = q.astype(BF16)
        qbs.append(qb)
        s = lax.dot_general(kb_scr[hh, pl.ds(own, tq), :], qb, NT, preferred_element_type=F32) * scale
        s = jnp.where(key <= qry, s, NEG)
        m = jnp.max(s, axis=0, keepdims=True)
        p = jnp.exp(s - m)
        m_scr[hh] = m
        l_scr[hh] = jnp.sum(p, axis=0, keepdims=True)
        acc_scr[hh] = jnp.dot(vt_scr[hh, i], p.astype(BF16), preferred_element_type=F32)

    def past_update(blocks):
        for hh in range(HEADS_PER_STEP):
            scores = []
            for n in blocks:
                sn = lax.dot_general(kb_scr[hh, n * tq:(n + 1) * tq, :], qbs[hh], NT,
                                     preferred_element_type=F32) * scale
                scores.append(jnp.where(sels[hh][n:n + 1, :] > 0, sn, NEG))
            m_old = m_scr[hh]
            m_new = m_old
            for sn in scores:
                m_new = jnp.maximum(m_new, jnp.max(sn, axis=0, keepdims=True))
            alpha = jnp.exp(m_old - m_new)
            l = alpha * l_scr[hh]
            acc = alpha * acc_scr[hh]
            for n, sn in zip(blocks, scores):
                pn = jnp.exp(sn - m_new)
                l = l + jnp.sum(pn, axis=0, keepdims=True)
                acc = acc + jnp.dot(vt_scr[hh, n], pn.astype(BF16), preferred_element_type=F32)
            l_scr[hh] = l
            acc_scr[hh] = acc
            m_scr[hh] = m_new

    _for_past_block_pairs(i, past_update)
    for hh in range(HEADS_PER_STEP):
        o_ref[:, _head_cols(hh)] = (acc_scr[hh] / l_scr[hh]).T


def _moba_prompt(proj):
    tq, hps = MOBA_BLOCK, HEADS_PER_STEP
    q_spec, k_spec, v_spec, o_spec = _prompt_specs(COL_MQ, COL_MK, COL_MV)
    return pl.pallas_call(
        _moba_prompt_kernel,
        grid=(BATCH, MOBA_HEADS // hps, N_PROMPT_BLOCKS),
        in_specs=[q_spec, k_spec, v_spec],
        out_specs=o_spec,
        out_shape=jax.ShapeDtypeStruct((N_PROMPT, MOBA_HEADS * HEAD_DIM), F32),
        scratch_shapes=[pltpu.VMEM((hps, SEQ, HEAD_DIM), BF16),
                        pltpu.VMEM((hps, N_PROMPT_BLOCKS, HEAD_DIM, tq), BF16),
                        pltpu.VMEM((hps, N_PROMPT_BLOCKS, HEAD_DIM), F32),
                        pltpu.VMEM((hps, 1, tq), F32), pltpu.VMEM((hps, 1, tq), F32),
                        pltpu.VMEM((hps, HEAD_DIM, tq), F32)],
        compiler_params=_cparams(("parallel", "parallel", "arbitrary"), 32),
        name="moba_prompt",
    )(proj, proj, proj)


def _diff_lambda(lq1_ref, lk1_ref, lq2_ref, lk2_ref, lam_init):
    a = jnp.sum(lq1_ref[...] * lk1_ref[...], axis=-1, keepdims=True)
    b = jnp.sum(lq2_ref[...] * lk2_ref[...], axis=-1, keepdims=True)
    return jnp.exp(a) - jnp.exp(b) + lam_init


def _diff_finish(o1, o2, lam, g, lam_init, axis=-1):
    o = o1 - lam * o2
    ms = jnp.mean(o * o, axis=axis, keepdims=True)
    return o * lax.rsqrt(ms + EPS) * g * (1.0 - lam_init)


def _split_maps(q):
    lane = lax.broadcasted_iota(jnp.int32, q.shape, 1)
    return jnp.where(lane < DIFF_QK_DIM, q, 0.0), jnp.where(lane >= DIFF_QK_DIM, q, 0.0)


def _diff_prompt_kernel(lq1_ref, lk1_ref, lq2_ref, lk2_ref, g_ref, q_ref, k_ref, v_ref, o_ref,
                        kb_scr, vt_scr, m_scr, l_scr, acc_scr, *, lam_init):
    i = pl.program_id(2)
    tq = MOBA_BLOCK
    scale = DIFF_QK_DIM ** -0.5

    @pl.when(i == 0)
    def _():
        _stage_kv(k_ref, v_ref, kb_scr, vt_scr)

    own = pl.multiple_of(i * tq, tq)
    key = lax.broadcasted_iota(jnp.int32, (tq, tq), 0)
    qry = lax.broadcasted_iota(jnp.int32, (tq, tq), 1)
    chains = []
    for hh in range(HEADS_PER_STEP):
        for c, qc in enumerate(_split_maps(q_ref[:, _head_cols(hh)])):
            chains.append((hh, 2 * hh + c, qc.astype(BF16)))
    for hh, slot, qc in chains:
        s = lax.dot_general(kb_scr[hh, pl.ds(own, tq), :], qc, NT, preferred_element_type=F32) * scale
        s = jnp.where(key <= qry, s, NEG)
        m = jnp.max(s, axis=0, keepdims=True)
        p = jnp.exp(s - m)
        m_scr[slot] = m
        l_scr[slot] = jnp.sum(p, axis=0, keepdims=True)
        acc_scr[slot] = jnp.dot(vt_scr[hh, i], p.astype(BF16), preferred_element_type=F32)

    def past_update(blocks):
        for hh, slot, qc in chains:
            scores = [lax.dot_general(kb_scr[hh, n * tq:(n + 1) * tq, :], qc, NT,
                                      preferred_element_type=F32) * scale for n in blocks]
            m_old = m_scr[slot]
            m_new = m_old
            for sn in scores:
                m_new = jnp.maximum(m_new, jnp.max(sn, axis=0, keepdims=True))
            alpha = jnp.exp(m_old - m_new)
            l = alpha * l_scr[slot]
            acc = alpha * acc_scr[slot]
            for n, sn in zip(blocks, scores):
                pn = jnp.exp(sn - m_new)
                l = l + jnp.sum(pn, axis=0, keepdims=True)
                acc = acc + jnp.dot(vt_scr[hh, n], pn.astype(BF16), preferred_element_type=F32)
            l_scr[slot] = l
            acc_scr[slot] = acc
            m_scr[slot] = m_new

    _for_past_block_pairs(i, past_update)
    lam = _diff_lambda(lq1_ref, lk1_ref, lq2_ref, lk2_ref, lam_init)
    for hh in range(HEADS_PER_STEP):
        y = _diff_finish(acc_scr[2 * hh] / l_scr[2 * hh], acc_scr[2 * hh + 1] / l_scr[2 * hh + 1], lam,
                         g_ref[...], lam_init, axis=0)
        o_ref[:, _head_cols(hh)] = y.T


def _diff_prompt(proj, lams, subln_g, lam_init):
    tq, hps = MOBA_BLOCK, HEADS_PER_STEP
    lam_specs = [pl.BlockSpec((1, DIFF_QK_DIM), lambda b, h, i: (0, 0))] * 4
    q_spec, k_spec, v_spec, o_spec = _prompt_specs(COL_DQ, COL_DK, COL_DV)
    return pl.pallas_call(
        functools.partial(_diff_prompt_kernel, lam_init=lam_init),
        grid=(BATCH, DIFF_HEADS // hps, N_PROMPT_BLOCKS),
        in_specs=lam_specs + [pl.BlockSpec((HEAD_DIM, 1), lambda b, h, i: (0, 0)), q_spec, k_spec, v_spec],
        out_specs=o_spec,
        out_shape=jax.ShapeDtypeStruct((N_PROMPT, DIFF_HEADS * HEAD_DIM), F32),
        scratch_shapes=[pltpu.VMEM((hps, SEQ, HEAD_DIM), BF16),
                        pltpu.VMEM((hps, N_PROMPT_BLOCKS, HEAD_DIM, tq), BF16),
                        pltpu.VMEM((2 * hps, 1, tq), F32), pltpu.VMEM((2 * hps, 1, tq), F32),
                        pltpu.VMEM((2 * hps, HEAD_DIM, tq), F32)],
        compiler_params=_cparams(("parallel", "parallel", "arbitrary"), 32),
        name="diff_prompt",
    )(*lams, subln_g.reshape(HEAD_DIM, 1), proj, proj, proj)


def _page_specs(layer, pinned_phase=None):
    specs = []
    for r in range(PAGES_PER_STEP):
        if pinned_phase is None:
            def imap(b, s, pt, r=r):
                return (layer, pt[b, s * PAGES_PER_STEP + r], 0, 0, 0)
        elif pinned_phase == 1:
            def imap(b, ph, s, pt, r=r):
                st = jnp.where(ph == 0, s, N_PAGE_STEPS - 1)
                return (layer, pt[b, st * PAGES_PER_STEP + r], 0, 0, 0)
        else:
            def imap(b, ph, s, pt, r=r):
                st = jnp.where(ph == 0, 0, s)
                return (layer, pt[b, st * PAGES_PER_STEP + r], 0, 0, 0)
        specs.append(pl.BlockSpec((None, None, MOBA_HEADS, PAGE_SIZE, HEAD_DIM), imap))
    return specs


def _head_major(cache):
    return jnp.transpose(cache, (0, 1, 3, 2, 4))


def _load_pages(refs, h):
    return jnp.concatenate([r[h] for r in refs], axis=0)


def _new_key_mask():
    r = lax.broadcasted_iota(jnp.int32, (SUBLANES, PAGE_SIZE), 0)
    j = lax.broadcasted_iota(jnp.int32, (SUBLANES, PAGE_SIZE), 1)
    return (j <= (r & (DEC_SEQ - 1))) & (j < DEC_SEQ)


def _stage_new_kv(kn_ref, vn_ref, knp_scr, vnp_scr, h):
    knp_scr[...] = jnp.zeros(knp_scr.shape, F32)
    vnp_scr[...] = jnp.zeros(vnp_scr.shape, F32)
    knp_scr[0:DEC_SEQ, :] = kn_ref[:, h * HEAD_DIM:(h + 1) * HEAD_DIM]
    vnp_scr[0:DEC_SEQ, :] = vn_ref[:, h * HEAD_DIM:(h + 1) * HEAD_DIM]


def _diff_sample_kernel(pt_ref, lq1_ref, lk1_ref, lq2_ref, lk2_ref, g_ref, q_ref, kn_ref, vn_ref, *rest,
                        lam_init):
    k_refs = rest[:PAGES_PER_STEP]
    v_refs = rest[PAGES_PER_STEP:2 * PAGES_PER_STEP]
    o_ref = rest[2 * PAGES_PER_STEP]
    q8_scr, knp_scr, vnp_scr, m_scr, l_scr, acc_scr = rest[2 * PAGES_PER_STEP + 1:]
    step = pl.program_id(1)
    scale = DIFF_QK_DIM ** -0.5

    @pl.when(step == 0)
    def _():
        for h in range(DIFF_HEADS):
            q1, q2 = _split_maps(q_ref[:, h * HEAD_DIM:(h + 1) * HEAD_DIM])
            q8_scr[h, 0:DEC_SEQ, :] = q1
            q8_scr[h, DEC_SEQ:2 * DEC_SEQ, :] = q2
        m_scr[...] = jnp.full(m_scr.shape, -jnp.inf, F32)
        l_scr[...] = jnp.zeros(l_scr.shape, F32)
        acc_scr[...] = jnp.zeros(acc_scr.shape, F32)

    def update(h, s, vb):
        m_old = m_scr[h]
        m_new = jnp.maximum(m_old, jnp.max(s, axis=-1, keepdims=True))
        alpha = jnp.exp(m_old - m_new)
        p = jnp.exp(s - m_new)
        l_scr[h] = alpha * l_scr[h] + jnp.sum(p, axis=-1, keepdims=True)
        acc_scr[h] = alpha * acc_scr[h] + jnp.dot(p.astype(BF16), vb, preferred_element_type=F32)
        m_scr[h] = m_new

    for h in range(DIFF_HEADS):
        q8 = q8_scr[h].astype(BF16)
        kb = _load_pages(k_refs, h).astype(BF16)
        vb = _load_pages(v_refs, h).astype(BF16)
        update(h, lax.dot_general(q8, kb, NT, preferred_element_type=F32) * scale, vb)

    @pl.when(step == N_PAGE_STEPS - 1)
    def _():
        lam = _diff_lambda(lq1_ref, lk1_ref, lq2_ref, lk2_ref, lam_init)
        for h in range(DIFF_HEADS):
            q8 = q8_scr[h].astype(BF16)
            _stage_new_kv(kn_ref, vn_ref, knp_scr, vnp_scr, h)
            sn = lax.dot_general(q8, knp_scr[...].astype(BF16), NT, preferred_element_type=F32) * scale
            update(h, jnp.where(_new_key_mask(), sn, NEG), vnp_scr[...].astype(BF16))
            o = acc_scr[h] / l_scr[h]
            o_ref[:, h * HEAD_DIM:(h + 1) * HEAD_DIM] = _diff_finish(
                o[0:DEC_SEQ, :], o[DEC_SEQ:2 * DEC_SEQ, :], lam, g_ref[...], lam_init)


def _tok_spec(c0):
    width = MOBA_HEADS * HEAD_DIM
    return pl.BlockSpec((None, DEC_SEQ, width), lambda b, *_: (b, 0, c0 // MOBA_HEADS))


def _diff_sample(proj3, cache_k, cache_v, page_table, layer, lams, subln_g, lam_init):
    lam_specs = [pl.BlockSpec((1, DIFF_QK_DIM), lambda b, s, pt: (0, 0))] * 4
    grid_spec = pltpu.PrefetchScalarGridSpec(
        num_scalar_prefetch=1,
        grid=(DEC_BATCH, N_PAGE_STEPS),
        in_specs=lam_specs + [pl.BlockSpec((1, HEAD_DIM), lambda b, s, pt: (0, 0)),
                              _tok_spec(COL_DQ), _tok_spec(COL_DK), _tok_spec(COL_DV)]
        + _page_specs(layer) + _page_specs(layer),
        out_specs=pl.BlockSpec((None, DEC_SEQ, DIFF_HEADS * HEAD_DIM), lambda b, s, pt: (b, 0, 0)),
        scratch_shapes=[pltpu.VMEM((DIFF_HEADS, SUBLANES, HEAD_DIM), F32),
                        pltpu.VMEM((PAGE_SIZE, HEAD_DIM), F32), pltpu.VMEM((PAGE_SIZE, HEAD_DIM), F32),
                        pltpu.VMEM((DIFF_HEADS, SUBLANES, 1), F32), pltpu.VMEM((DIFF_HEADS, SUBLANES, 1), F32),
                        pltpu.VMEM((DIFF_HEADS, SUBLANES, HEAD_DIM), F32)])
    return pl.pallas_call(
        functools.partial(_diff_sample_kernel, lam_init=lam_init),
        grid_spec=grid_spec,
        out_shape=jax.ShapeDtypeStruct((DEC_BATCH, DEC_SEQ, DIFF_HEADS * HEAD_DIM), F32),
        compiler_params=_cparams(("parallel", "arbitrary"), 56),
        name="diff_sample",
    )(page_table, *lams, subln_g.reshape(1, HEAD_DIM), proj3, proj3, proj3,
      *([cache_k] * PAGES_PER_STEP), *([cache_v] * PAGES_PER_STEP))


def _moba_sample_kernel(pt_ref, q_ref, kn_ref, vn_ref, *rest):
    k_refs = rest[:PAGES_PER_STEP]
    v_refs = rest[PAGES_PER_STEP:2 * PAGES_PER_STEP]
    o_ref = rest[2 * PAGES_PER_STEP]
    q8_scr, knp_scr, vnp_scr, s_scr, km_scr, p_scr, l_scr, acc_scr = rest[2 * PAGES_PER_STEP + 1:]
    phase = pl.program_id(1)
    step = pl.program_id(2)
    scale = HEAD_DIM ** -0.5

    @pl.when((phase == 0) & (step == 0))
    def _():
        q8_scr[...] = jnp.zeros(q8_scr.shape, F32)
        for h in range(MOBA_HEADS):
            q8_scr[h, 0:DEC_SEQ, :] = q_ref[:, h * HEAD_DIM:(h + 1) * HEAD_DIM]

    @pl.when(phase == 0)
    def _():
        for h in range(MOBA_HEADS):
            k = _load_pages(k_refs, h)
            km_scr[h, step] = (jnp.sum(k.reshape(BLOCKS_PER_STEP, MOBA_BLOCK, HEAD_DIM), axis=1)
                               * (1.0 / MOBA_BLOCK))
            s_scr[h, step] = lax.dot_general(q8_scr[h].astype(BF16), k.astype(BF16), NT,
                                             preferred_element_type=F32) * scale

    @pl.when((phase == 1) & (step == 0))
    def _():
        for h in range(MOBA_HEADS):
            q8 = q8_scr[h]
            km = km_scr[h].reshape(N_PAST_BLOCKS, HEAD_DIM)
            gate = lax.dot_general(q8, km, NT, precision=HI, preferred_element_type=F32)
            sel = _rank_lt(gate, N_PAST_BLOCKS, MOBA_TOPK, axis=1)
            _stage_new_kv(kn_ref, vn_ref, knp_scr, vnp_scr, h)
            sn = lax.dot_general(q8.astype(BF16), knp_scr[...].astype(BF16), NT,
                                 preferred_element_type=F32) * scale
            sn = jnp.where(_new_key_mask(), sn, NEG)
            m = jnp.max(sn, axis=-1, keepdims=True)

            def block(n, h=h, sel=sel):
                st, jb = divmod(n, BLOCKS_PER_STEP)
                return jnp.where(sel[:, n:n + 1] > 0,
                                 s_scr[h, st, :, jb * MOBA_BLOCK:(jb + 1) * MOBA_BLOCK], NEG)

            for n in range(N_PAST_BLOCKS):
                m = jnp.maximum(m, jnp.max(block(n), axis=-1, keepdims=True))
            pn = jnp.exp(sn - m)
            l = jnp.sum(pn, axis=-1, keepdims=True)
            for n in range(N_PAST_BLOCKS):
                st, jb = divmod(n, BLOCKS_PER_STEP)
                pb = jnp.exp(block(n) - m)
                l = l + jnp.sum(pb, axis=-1, keepdims=True)
                p_scr[h, st, :, jb * MOBA_BLOCK:(jb + 1) * MOBA_BLOCK] = pb
            l_scr[h] = l
            acc_scr[h] = jnp.dot(pn.astype(BF16), vnp_scr[...].astype(BF16), preferred_element_type=F32)

    @pl.when(phase == 1)
    def _():
        for h in range(MOBA_HEADS):
            vb = _load_pages(v_refs, h).astype(BF16)
            acc_scr[h] += jnp.dot(p_scr[h, step].astype(BF16), vb, preferred_element_type=F32)

    @pl.when((phase == 1) & (step == N_PAGE_STEPS - 1))
    def _():
        for h in range(MOBA_HEADS):
            o = acc_scr[h] / l_scr[h]
            o_ref[:, h * HEAD_DIM:(h + 1) * HEAD_DIM] = o[0:DEC_SEQ, :]


def _moba_sample(proj3, cache_k, cache_v, page_table, layer):
    grid_spec = pltpu.PrefetchScalarGridSpec(
        num_scalar_prefetch=1,
        grid=(DEC_BATCH, 2, N_PAGE_STEPS),
        in_specs=[_tok_spec(COL_MQ), _tok_spec(COL_MK), _tok_spec(COL_MV)]
        + _page_specs(layer, pinned_phase=1) + _page_specs(layer, pinned_phase=0),
        out_specs=pl.BlockSpec((None, DEC_SEQ, MOBA_HEADS * HEAD_DIM), lambda b, ph, s, pt: (b, 0, 0)),
        scratch_shapes=[pltpu.VMEM((MOBA_HEADS, SUBLANES, HEAD_DIM), F32),
                        pltpu.VMEM((PAGE_SIZE, HEAD_DIM), F32), pltpu.VMEM((PAGE_SIZE, HEAD_DIM), F32),
                        pltpu.VMEM((MOBA_HEADS, N_PAGE_STEPS, SUBLANES, KEYS_PER_STEP), F32),
                        pltpu.VMEM((MOBA_HEADS, N_PAGE_STEPS, BLOCKS_PER_STEP, HEAD_DIM), F32),
                        pltpu.VMEM((MOBA_HEADS, N_PAGE_STEPS, SUBLANES, KEYS_PER_STEP), F32),
                        pltpu.VMEM((MOBA_HEADS, SUBLANES, 1), F32),
                        pltpu.VMEM((MOBA_HEADS, SUBLANES, HEAD_DIM), F32)])
    return pl.pallas_call(
        _moba_sample_kernel,
        grid_spec=grid_spec,
        out_shape=jax.ShapeDtypeStruct((DEC_BATCH, DEC_SEQ, MOBA_HEADS * HEAD_DIM), F32),
        compiler_params=_cparams(("parallel", "arbitrary", "arbitrary"), 56),
        name="moba_sample",
    )(page_table, proj3, proj3, proj3, *([cache_k] * PAGES_PER_STEP), *([cache_v] * PAGES_PER_STEP))


def _out_proj_kernel(x_ref, c_ref, m_ref, d_ref, wc_ref, wm_ref, wd_ref, o_ref):
    acc = jnp.dot(c_ref[...].astype(BF16), wc_ref[...], preferred_element_type=F32)
    acc += jnp.dot(m_ref[...].astype(BF16), wm_ref[...], preferred_element_type=F32)
    acc += jnp.dot(d_ref[...].astype(BF16), wd_ref[...], preferred_element_type=F32)
    o_ref[...] = x_ref[...] + acc


def _out_proj(x, conv_o, moba_o, diff_o, wc, wm, wd, tm):
    t = x.shape[0]
    row = lambda w: pl.BlockSpec((tm, w), lambda i: (i, 0))
    full = lambda a: pl.BlockSpec(a.shape, lambda i: (0, 0))
    return pl.pallas_call(
        _out_proj_kernel,
        grid=(t // tm,),
        in_specs=[row(D_MODEL), row(conv_o.shape[1]), row(moba_o.shape[1]), row(diff_o.shape[1]),
                  full(wc), full(wm), full(wd)],
        out_specs=row(D_MODEL),
        out_shape=jax.ShapeDtypeStruct((t, D_MODEL), F32),
        compiler_params=_cparams(("parallel",), 48),
        name="out_proj",
    )(x, conv_o, moba_o, diff_o, wc, wm, wd)


def _top16_rows(s, row_id, id_bound):
    t = s.shape[1]
    slot = lax.broadcasted_iota(jnp.int32, (PEER_TOPK, t), 0)

    def body(k, carry):
        s, vals, idxs = carry
        m = jnp.max(s, axis=0, keepdims=True)
        idx = jnp.min(jnp.where(s == m, row_id, id_bound), axis=0, keepdims=True)
        vals = jnp.where(slot == k, m, vals)
        idxs = jnp.where(slot == k, idx, idxs)
        s = jnp.where(row_id == idx, -jnp.inf, s)
        return s, vals, idxs

    _, vals, idxs = lax.fori_loop(0, PEER_TOPK, body,
                                  (s, jnp.zeros((PEER_TOPK, t), F32), jnp.zeros((PEER_TOPK, t), jnp.int32)))
    return vals, idxs


def _route_kernel(q_ref, keys_ref, i1_ref, i2_ref, g_ref):
    t = q_ref.shape[0]
    key_id = lax.broadcasted_iota(jnp.int32, (PEER_N_KEYS, t), 0)
    r16 = lax.broadcasted_iota(jnp.int32, (PEER_TOPK, t), 0)
    r8 = lax.broadcasted_iota(jnp.int32, (SUBLANES, t), 0)
    pair_id = jnp.concatenate([r16] + [k1 * PEER_TOPK + r8 for k1 in range(1, SUBLANES)]
                              + [(r8 + SUBLANES) * PEER_TOPK], axis=0)
    i1_parts, i2_parts, g_parts = [], [], []
    for h in range(PEER_HEADS):
        tops = []
        for p in range(2):
            c0 = (h * 2 + p) * PEER_N_KEYS
            q_hp = q_ref[:, c0:c0 + PEER_N_KEYS]
            s_t = lax.dot_general(keys_ref[h, p], q_hp, NT, precision=HI, preferred_element_type=F32)
            tops.append(_top16_rows(s_t, key_id, PEER_N_KEYS))
        (v0, x0), (v1, x1) = tops
        cand = jnp.concatenate([v0[0:1, :] + v1]
                               + [v0[k1:k1 + 1, :] + v1[0:SUBLANES, :] for k1 in range(1, SUBLANES)]
                               + [v0[SUBLANES:PEER_TOPK, :] + v1[0:1, :]], axis=0)
        best, pos = _top16_rows(cand, pair_id, PEER_TOPK * PEER_TOPK)
        k1 = pos >> 4
        k2 = pos & (PEER_TOPK - 1)
        a = jnp.zeros(pos.shape, jnp.int32)
        b = jnp.zeros(pos.shape, jnp.int32)
        for k in range(PEER_TOPK):
            a = jnp.where(k1 == k, x0[k:k + 1, :], a)
            b = jnp.where(k2 == k, x1[k:k + 1, :], b)
        e = jnp.exp(best - jnp.max(best, axis=0, keepdims=True))
        g_parts.append(e / jnp.sum(e, axis=0, keepdims=True))
        i1_parts.append(a)
        i2_parts.append(b)
    i1_ref[...] = jnp.concatenate(i1_parts, axis=0).T
    i2_ref[...] = jnp.concatenate(i2_parts, axis=0).T
    g_ref[...] = jnp.concatenate(g_parts, axis=0).T


def _route(q, keys):
    t = q.shape[0]
    tb = min(4 * LANES, t)
    slots = PEER_HEADS * PEER_TOPK
    spec = pl.BlockSpec((tb, slots), lambda i: (i, 0))
    return pl.pallas_call(
        _route_kernel,
        grid=(t // tb,),
        in_specs=[pl.BlockSpec((tb, q.shape[1]), lambda i: (i, 0)),
                  pl.BlockSpec(keys.shape, lambda i: (0, 0, 0, 0))],
        out_specs=[spec, spec, spec],
        out_shape=[jax.ShapeDtypeStruct((t, slots), jnp.int32), jax.ShapeDtypeStruct((t, slots), jnp.int32),
                   jax.ShapeDtypeStruct((t, slots), F32)],
        compiler_params=_cparams(("parallel",), 32),
        name="peer_route",
    )(q, keys)


def _expert_kernel(x_ref, h_ref, i1_ref, i2_ref, g_ref, gf_ref, u_ref, v_ref, o_ref, w_scr, acc_scr, *,
                   final_norm):
    c = pl.program_id(1)
    tb = x_ref.shape[0]
    n_sub = PEER_EXPERT_CHUNK // PEER_N_KEYS
    w_stride = tb + W_PAD

    @pl.when(c == 0)
    def _():
        sub = lax.broadcasted_iota(jnp.int32, (PEER_N_KEYS, PEER_N_KEYS), 0)

        def build(t, carry):
            i1_row = i1_ref[pl.ds(t, 1), :]
            i2_row = i2_ref[pl.ds(t, 1), :]
            g_row = g_ref[pl.ds(t, 1), :]
            p_t = jnp.where(sub == i1_row, g_row, 0.0).astype(BF16)
            q_t = jnp.where(sub == i2_row, 1.0, 0.0).astype(BF16)
            w_t = lax.dot_general(p_t, q_t, NT, preferred_element_type=F32)
            w_scr[pl.ds(t, PEER_N_KEYS, stride=w_stride), :] = w_t
            return carry

        lax.fori_loop(0, tb, build, 0, unroll=32)
        acc_scr[...] = jnp.zeros(acc_scr.shape, F32)

    act = lax.dot_general(h_ref[...], u_ref[...], NT, preferred_element_type=F32)
    gel = 0.5 * act * (1.0 + lax.erf(act * math.sqrt(0.5)))
    parts = []
    for a in range(n_sub):
        w_a = w_scr[pl.ds(pl.multiple_of((c * n_sub + a) * w_stride, SUBLANES), tb), :]
        parts.append((w_a * gel[:, a * PEER_N_KEYS:(a + 1) * PEER_N_KEYS]).astype(BF16))
    wa = jnp.concatenate(parts, axis=1)
    acc_scr[...] += jnp.dot(wa, v_ref[...], preferred_element_type=F32)

    @pl.when(c == pl.num_programs(1) - 1)
    def _():
        y = x_ref[...] + acc_scr[...]
        if final_norm:
            ms = jnp.mean(y * y, axis=-1, keepdims=True)
            y = y * lax.rsqrt(ms + EPS) * gf_ref[...]
        o_ref[...] = y


def _experts(x, h, i1, i2, g, g_final, u, v, layer, *, tb, final_norm):
    t = x.shape[0]
    slots = i1.shape[1]
    row = lambda w: pl.BlockSpec((tb, w), lambda i, c: (i, 0))
    chunk = pl.BlockSpec((None, PEER_EXPERT_CHUNK, D_MODEL), lambda i, c: (layer, c, 0))
    return pl.pallas_call(
        functools.partial(_expert_kernel, final_norm=final_norm),
        grid=(t // tb, PEER_EXPERTS // PEER_EXPERT_CHUNK),
        in_specs=[row(D_MODEL), row(D_MODEL), row(slots), row(slots), row(slots),
                  pl.BlockSpec((1, D_MODEL), lambda i, c: (0, 0)), chunk, chunk],
        out_specs=row(D_MODEL),
        out_shape=jax.ShapeDtypeStruct((t, D_MODEL), F32),
        scratch_shapes=[pltpu.VMEM((PEER_N_KEYS * (tb + W_PAD), PEER_N_KEYS), F32),
                        pltpu.VMEM((tb, D_MODEL), F32)],
        compiler_params=_cparams(("parallel", "arbitrary"), 56),
        name="peer_experts",
    )(x, h, i1, i2, g, g_final.reshape(1, D_MODEL), u, v)


def _layer_tail(x, conv_o, moba_o, diff_o, lw, g_final, *, tm, tb, final_norm):
    x1 = _out_proj(x, conv_o, moba_o, diff_o, lw["wo_c"], lw["wo_m"], lw["wo_d"], tm=min(tm, 256))
    q, h2 = _project(x1, lw["g_ffn"], lw["wq"], lw["layer"], emit_h=True, tm=tm, tn=1024)
    i1, i2, g = _route(q, lw["keys"])
    return _experts(x1, h2, i1, i2, g, g_final, lw["u"], lw["v"], 0, tb=tb, final_norm=final_norm)


def kernel(x_prompt, x_sample, cache_moba_k, cache_moba_v, cache_diff_k, cache_diff_v, state_conv, page_table,
           g_mix, w_in, conv_w, diff_lambda_q1, diff_lambda_k1, diff_lambda_q2, diff_lambda_k2, diff_subln_g,
           w_out, g_ffn, peer_w_query, peer_sub_keys, peer_expert_u, peer_expert_v, g_final):
    xp = x_prompt.reshape(N_PROMPT, D_MODEL)
    xs = x_sample.reshape(N_SAMPLE, D_MODEL)
    cache_moba_k, cache_moba_v, cache_diff_k, cache_diff_v = (
        _head_major(c) for c in (cache_moba_k, cache_moba_v, cache_diff_k, cache_diff_v))
    prompt_state, sample_state = [], []
    w_in_b, wq_b = w_in.astype(BF16), peer_w_query.astype(BF16)
    for l in range(DEPTH):
        lam_init = 0.8 - 0.6 * math.exp(-0.3 * l)
        final = l == DEPTH - 1
        wo = w_out[l].astype(BF16)
        u_l, v_l = (lax.optimization_barrier(w[l:l + 1]).astype(BF16) for w in (peer_expert_u, peer_expert_v))
        lw = dict(wo_c=wo[:CONV_CH], wo_m=wo[CONV_CH:CONV_CH + MOBA_HEADS * HEAD_DIM],
                  wo_d=wo[CONV_CH + MOBA_HEADS * HEAD_DIM:], g_ffn=g_ffn[l], wq=wq_b,
                  keys=peer_sub_keys[l], u=u_l, v=v_l, layer=l)
        lams = [a[l].reshape(1, DIFF_QK_DIM) for a in
                (diff_lambda_q1, diff_lambda_k1, diff_lambda_q2, diff_lambda_k2)]

        proj, *new_kv = _project(xp, g_mix[l], w_in_b, l, seq=SEQ, tm=512, tn=STATE_WIDTH)
        conv_o, conv_new = _conv_prompt(proj, conv_w[l])
        moba_o = _moba_prompt(proj)
        diff_o = _diff_prompt(proj, lams, diff_subln_g[l], lam_init)
        xp = _layer_tail(xp, conv_o, moba_o, diff_o, lw, g_final, tm=512, tb=256, final_norm=final)
        prompt_state.append((new_kv, conv_new))

        proj_s = _project(xs, g_mix[l], w_in_b, l, tm=N_SAMPLE, tn=1024)
        proj3 = proj_s.reshape(DEC_BATCH, DEC_SEQ, proj_s.shape[1])
        conv_o, conv_new = _conv_sample(proj_s, conv_w[l], state_conv[l])
        moba_o = _moba_sample(proj3, cache_moba_k, cache_moba_v, page_table, l)
        diff_o = _diff_sample(proj3, cache_diff_k, cache_diff_v, page_table, l, lams, diff_subln_g[l], lam_init)
        xs = _layer_tail(xs, conv_o, moba_o.reshape(N_SAMPLE, -1), diff_o.reshape(N_SAMPLE, -1), lw, g_final,
                         tm=N_SAMPLE, tb=N_SAMPLE, final_norm=final)
        sample_state.append((proj_s, conv_new))

    def sample_states(group):
        cols = lambda c0: jnp.stack([p[:, c0 * LANES:(c0 + MOBA_HEADS) * LANES]
                                     .reshape(DEC_BATCH, DEC_SEQ, MOBA_HEADS, HEAD_DIM) for p, _ in group])
        return cols(COL_MK), cols(COL_MV), cols(COL_DK), cols(COL_DV), jnp.stack([c for _, c in group])

    def prompt_states(group):
        kv = tuple(jnp.transpose(jnp.stack([s[n] for s, _ in group]), (0, 1, 3, 2, 4))
                   for n in range(len(STATE_TILES)))
        return kv + (jnp.stack([c for _, c in group]),)

    y_prompt = xp.reshape(BATCH, SEQ, D_MODEL)
    y_sample = xs.reshape(DEC_BATCH, DEC_SEQ, D_MODEL)
    return (y_prompt, y_sample) + prompt_states(prompt_state) + sample_states(sample_state)
```
